```python
import math
import jax
import jax.numpy as jnp
from jax import lax
import numpy as np

D_MODEL = 2048
BATCH = 8
SEQ = 2048
DEPTH = 1
DEC_BATCH = 32
DEC_SEQ = 8
PAST_LEN = 16384
PAGE_SIZE = 128

SSM_WIDTH = 1024
SSM_GROUP_SIZE = 16
SSM_GROUPS = SSM_WIDTH // SSM_GROUP_SIZE
SSM_STATE = 64
DT_MIN = 0.001
DT_MAX = 0.1

HEAD_DIM = 128
HEADS_PER_GROUP = 4
WINDOWS = (128, 512, 2048)
DILATIONS = (1, 4, 16)
N_DIL_GROUPS = len(WINDOWS)
N_ATTN_HEADS = N_DIL_GROUPS * HEADS_PER_GROUP
QKV_WIDTH = N_ATTN_HEADS * HEAD_DIM
ATTN_OUT_WIDTH = HEADS_PER_GROUP * HEAD_DIM
ATTN_Q_BLOCK = 128

IN_SPLITS = (SSM_WIDTH, SSM_WIDTH + QKV_WIDTH, SSM_WIDTH + 2 * QKV_WIDTH,
             SSM_WIDTH + 3 * QKV_WIDTH, SSM_WIDTH + 3 * QKV_WIDTH + D_MODEL)
IN_WIDTH = SSM_WIDTH + 3 * QKV_WIDTH + 2 * D_MODEL

PEER_HEADS = 8
PEER_N_KEYS = 128
PEER_EXPERTS = PEER_N_KEYS ** 2
PEER_TOPK = 16
PEER_HALF_DIM = 128
PEER_QUERY_DIM = 2 * PEER_HALF_DIM
PEER_TOKEN_BLOCK = 128

DEEPNORM_ALPHA = (2.0 * DEPTH) ** 0.25
DEEPNORM_BETA = (8.0 * DEPTH) ** -0.25
LN_EPS = 1e-5

kernel_name = "dilated_attn_s5_peer_hybrid_step"


def _layer_norm(x, g, b):
    xf = x.astype(jnp.float32)
    mu = jnp.mean(xf, axis=-1, keepdims=True)
    var = jnp.mean(jnp.square(xf - mu), axis=-1, keepdims=True)
    y = (xf - mu) * lax.rsqrt(var + LN_EPS) * g.astype(jnp.float32) + b.astype(jnp.float32)
    return y.astype(x.dtype)


def _complex_linear_combine(e1, e2):
    ar1, ai1, br1, bi1 = e1
    ar2, ai2, br2, bi2 = e2
    return (ar1 * ar2 - ai1 * ai2,
            ar1 * ai2 + ai1 * ar2,
            ar2 * br1 - ai2 * bi1 + br2,
            ar2 * bi1 + ai2 * br1 + bi2)


def _s5_branch(u, h0, log_dt, a_re, a_im, b_re, b_im, c_re, c_im, d_skip, w_glu):
    f32 = jnp.float32
    bsz, n_pos, _ = u.shape
    uf = u.astype(f32)
    ar = a_re.astype(f32)
    ai = a_im.astype(f32)
    dt = jnp.exp(log_dt.astype(f32))[:, None]
    mag = jnp.exp(ar * dt)
    abar_re = mag * jnp.cos(ai * dt)
    abar_im = mag * jnp.sin(ai * dt)
    den = ar * ar + ai * ai
    f_re = ((abar_re - 1.0) * ar + abar_im * ai) / den
    f_im = (abar_im * ar - (abar_re - 1.0) * ai) / den
    br = b_re.astype(f32)
    bi = b_im.astype(f32)
    bb_re = f_re[..., None] * br - f_im[..., None] * bi
    bb_im = f_re[..., None] * bi + f_im[..., None] * br
    ug = uf.reshape(bsz, n_pos, SSM_GROUPS, SSM_GROUP_SIZE)
    bu_re = jnp.einsum('blgh,gph->lbgp', ug, bb_re)
    bu_im = jnp.einsum('blgh,gph->lbgp', ug, bb_im)
    if h0 is not None:
        h0r = h0[..., 0].astype(f32)
        h0i = h0[..., 1].astype(f32)
        bu_re = bu_re.at[0].add(abar_re * h0r - abar_im * h0i)
        bu_im = bu_im.at[0].add(abar_re * h0i + abar_im * h0r)
    a_seq_re = jnp.broadcast_to(abar_re, (n_pos, 1, SSM_GROUPS, SSM_STATE))
    a_seq_im = jnp.broadcast_to(abar_im, (n_pos, 1, SSM_GROUPS, SSM_STATE))
    _, _, h_re, h_im = lax.associative_scan(
        _complex_linear_combine, (a_seq_re, a_seq_im, bu_re, bu_im), axis=0)
    y = (jnp.einsum('lbgp,ghp->blgh', h_re, c_re.astype(f32))
         - jnp.einsum('lbgp,ghp->blgh', h_im, c_im.astype(f32)))
    y = y.reshape(bsz, n_pos, SSM_WIDTH) + d_skip.astype(f32) * uf
    g = jax.nn.gelu(y)
    out = g * jax.nn.sigmoid(g @ w_glu.astype(f32))
    final = jnp.stack([h_re[-1], h_im[-1]], axis=-1)
    return out.astype(u.dtype), final.astype(u.dtype)


def _alibi_slopes():
    return 2.0 ** (-8.0 * jnp.arange(1, N_ATTN_HEADS + 1, dtype=jnp.float32) / N_ATTN_HEADS)


def _dilated_mixture(q, k_pads, v_pads, local, qpos):
    scale = HEAD_DIM ** -0.5
    slopes = _alibi_slopes()
    outs, lses = [], []
    for g in range(N_DIL_GROUPS):
        d = DILATIONS[g]
        offs = jnp.arange(WINDOWS[g] // d + 1, dtype=jnp.int32) * d
        idx = WINDOWS[g] + local[:, None] - offs[None, :]
        valid = (qpos[:, None] - offs[None, :]) >= 0
        kg = jnp.take(k_pads[g], idx, axis=1)
        vg = jnp.take(v_pads[g], idx, axis=1)
        qg = q[:, :, g * HEADS_PER_GROUP:(g + 1) * HEADS_PER_GROUP]
        s = jnp.einsum('bqhd,bqkhd->bhqk', qg, kg, preferred_element_type=jnp.float32) * scale
        s = s - slopes[g * HEADS_PER_GROUP:(g + 1) * HEADS_PER_GROUP, None, None] * offs.astype(jnp.float32)
        s = jnp.where(valid, s, -jnp.inf)
        lse = jax.nn.logsumexp(s, axis=-1)
        p = jnp.exp(s - lse[..., None])
        outs.append(jnp.einsum('bhqk,bqkhd->bqhd', p, vg.astype(jnp.float32)))
        lses.append(lse)
    w = jax.nn.softmax(jnp.stack(lses, axis=0), axis=0)
    w = jnp.transpose(w, (0, 1, 3, 2))[..., None]
    out = jnp.sum(w * jnp.stack(outs, axis=0), axis=0)
    return out.astype(q.dtype)


def _split_group(t, g):
    return t[:, :, g * HEADS_PER_GROUP:(g + 1) * HEADS_PER_GROUP]


def _prompt_attention(q, k, v):
    bsz, n_pos = q.shape[0], q.shape[1]
    k_pads, v_pads, new_kv = [], [], []
    for g in range(N_DIL_GROUPS):
        kg = _split_group(k, g)
        vg = _split_group(v, g)
        pad = ((0, 0), (WINDOWS[g], 0), (0, 0), (0, 0))
        k_pads.append(jnp.pad(kg, pad))
        v_pads.append(jnp.pad(vg, pad))
        keep = min(WINDOWS[g], n_pos)
        new_kv.append(jnp.stack([kg[:, n_pos - keep:], vg[:, n_pos - keep:]], axis=2))

    def block(t0):
        qb = lax.dynamic_slice_in_dim(q, t0, ATTN_Q_BLOCK, axis=1)
        pos = t0 + jnp.arange(ATTN_Q_BLOCK, dtype=jnp.int32)
        return _dilated_mixture(qb, k_pads, v_pads, pos, pos)

    starts = jnp.arange(n_pos // ATTN_Q_BLOCK, dtype=jnp.int32) * ATTN_Q_BLOCK
    o = lax.map(block, starts)
    o = jnp.transpose(o, (1, 0, 2, 3, 4)).reshape(bsz, n_pos, ATTN_OUT_WIDTH)
    return o, new_kv


def _sample_attention(q, k, v, kv_caches):
    bsz, n_new = q.shape[0], q.shape[1]
    k_pads, v_pads, new_kv = [], [], []
    for g in range(N_DIL_GROUPS):
        kg = _split_group(k, g)
        vg = _split_group(v, g)
        buf = kv_caches[g]
        fill = jnp.zeros((bsz, WINDOWS[g] - buf.shape[1], HEADS_PER_GROUP, HEAD_DIM), k.dtype)
        k_pads.append(jnp.concatenate([fill, buf[:, :, 0].astype(k.dtype), kg], axis=1))
        v_pads.append(jnp.concatenate([fill, buf[:, :, 1].astype(v.dtype), vg], axis=1))
        new_kv.append(jnp.stack([kg, vg], axis=2))
    local = jnp.arange(n_new, dtype=jnp.int32)
    o = _dilated_mixture(q, k_pads, v_pads, local, PAST_LEN + local)
    return o.reshape(bsz, n_new, ATTN_OUT_WIDTH), new_kv


def _peer(x2d, w_q, sub_keys_1, sub_keys_2, u_tab, v_tab):
    n_tok = x2d.shape[0]
    pad = (-n_tok) % PEER_TOKEN_BLOCK
    xp = jnp.pad(x2d, ((0, pad), (0, 0))).reshape(-1, PEER_TOKEN_BLOCK, D_MODEL)
    k1 = sub_keys_1.astype(jnp.float32)
    k2 = sub_keys_2.astype(jnp.float32)

    def block(xb):
        q = jnp.dot(xb, w_q, preferred_element_type=jnp.float32)
        q = q.reshape(PEER_TOKEN_BLOCK, PEER_HEADS, 2, PEER_HALF_DIM)
        s1 = jnp.einsum('thd,kd->thk', q[:, :, 0], k1)
        s2 = jnp.einsum('thd,kd->thk', q[:, :, 1], k2)
        v1, i1 = lax.top_k(s1, PEER_TOPK)
        v2, i2 = lax.top_k(s2, PEER_TOPK)
        cand = (v1[..., :, None] + v2[..., None, :]).reshape(PEER_TOKEN_BLOCK, PEER_HEADS, -1)
        cidx = (i1[..., :, None] * PEER_N_KEYS + i2[..., None, :]).reshape(PEER_TOKEN_BLOCK, PEER_HEADS, -1)
        top, pos = lax.top_k(cand, PEER_TOPK)
        eidx = jnp.take_along_axis(cidx, pos, axis=-1)
        gate = jax.nn.softmax(top, axis=-1)
        ug = u_tab[eidx]
        act = jax.nn.gelu(jnp.einsum('td,thkd->thk', xb, ug, preferred_element_type=jnp.float32))
        vg = v_tab[eidx]
        out = jnp.einsum('thk,thkd->td', gate * act, vg.astype(jnp.float32))
        return out.astype(xb.dtype)

    out = lax.map(block, xp).reshape(-1, D_MODEL)
    return out[:n_tok]


def _decoder_layer(x, kv_caches, h0, w_in, ssm_log_dt, ssm_a_re, ssm_a_im, ssm_b_re, ssm_b_im,
                   ssm_c_re, ssm_c_im, ssm_d, ssm_w_glu, w_branch_ssm, w_branch_attn, w_out,
                   ln1_g, ln1_b, peer_w_q, peer_sub_keys_1, peer_sub_keys_2, peer_u, peer_v,
                   ln2_g, ln2_b):
    bsz, n_pos, _ = x.shape
    z = jnp.einsum('bld,dn->bln', x, w_in)
    u, q, k, v, gate_s, gate_a = jnp.split(z, IN_SPLITS, axis=-1)
    q = q.reshape(bsz, n_pos, N_ATTN_HEADS, HEAD_DIM)
    k = k.reshape(bsz, n_pos, N_ATTN_HEADS, HEAD_DIM)
    v = v.reshape(bsz, n_pos, N_ATTN_HEADS, HEAD_DIM)
    ssm_out, ssm_state = _s5_branch(u, h0, ssm_log_dt, ssm_a_re, ssm_a_im, ssm_b_re, ssm_b_im,
                                    ssm_c_re, ssm_c_im, ssm_d, ssm_w_glu)
    if kv_caches is None:
        attn_out, new_kv = _prompt_attention(q, k, v)
    else:
        attn_out, new_kv = _sample_attention(q, k, v, kv_caches)
    merged = (jax.nn.sigmoid(gate_s) * jnp.einsum('bln,nd->bld', ssm_out, w_branch_ssm)
              + jax.nn.sigmoid(gate_a) * jnp.einsum('bln,nd->bld', attn_out, w_branch_attn))
    mix = jnp.einsum('bld,de->ble', merged, w_out)
    h = _layer_norm(DEEPNORM_ALPHA * x + mix, ln1_g, ln1_b)
    f = _peer(h.reshape(-1, D_MODEL), peer_w_q, peer_sub_keys_1, peer_sub_keys_2,
              peer_u, peer_v).reshape(bsz, n_pos, D_MODEL)
    y = _layer_norm(DEEPNORM_ALPHA * h + f, ln2_g, ln2_b)
    return y, new_kv, ssm_state


def setup_inputs(seed: int = 0) -> dict:
    key = jax.random.key(seed)
    ks = jax.random.split(key, 32)
    f32 = jnp.float32

    def nrm(k, shape, s):
        return jax.random.normal(k, shape, f32) * s

    kv_shape = lambda w: (DEC_BATCH, min(w, PAST_LEN), 2, HEADS_PER_GROUP, HEAD_DIM)
    col_scale = jnp.concatenate([
        jnp.ones((SSM_WIDTH + 2 * QKV_WIDTH,), f32),
        jnp.full((QKV_WIDTH,), DEEPNORM_BETA, f32),
        jnp.ones((2 * D_MODEL,), f32)])
    return {
        'x_prompt': nrm(ks[0], (BATCH, SEQ, D_MODEL), 1.0),
        'x_sample': nrm(ks[1], (DEC_BATCH, DEC_SEQ, D_MODEL), 1.0),
        'cache_kv_w128': nrm(ks[2], kv_shape(WINDOWS[0]), 1.0),
        'cache_kv_w512': nrm(ks[3], kv_shape(WINDOWS[1]), 1.0),
        'cache_kv_w2048': nrm(ks[4], kv_shape(WINDOWS[2]), 1.0),
        'state_ssm': nrm(ks[5], (DEC_BATCH, SSM_GROUPS, SSM_STATE, 2), 0.5),
        'w_in': nrm(ks[6], (D_MODEL, IN_WIDTH), D_MODEL ** -0.5) * col_scale,
        'ssm_log_dt': jax.random.uniform(ks[7], (SSM_GROUPS,), f32, math.log(DT_MIN), math.log(DT_MAX)),
        'ssm_a_re': -0.5 * jnp.exp(nrm(ks[8], (SSM_GROUPS, SSM_STATE), 0.01)),
        'ssm_a_im': math.pi * jnp.arange(SSM_STATE, dtype=f32)[None, :] + nrm(ks[9], (SSM_GROUPS, SSM_STATE), 0.01),
        'ssm_b_re': nrm(ks[10], (SSM_GROUPS, SSM_STATE, SSM_GROUP_SIZE), (2 * SSM_GROUP_SIZE) ** -0.5),
        'ssm_b_im': nrm(ks[11], (SSM_GROUPS, SSM_STATE, SSM_GROUP_SIZE), (2 * SSM_GROUP_SIZE) ** -0.5),
        'ssm_c_re': nrm(ks[12], (SSM_GROUPS, SSM_GROUP_SIZE, SSM_STATE), SSM_STATE ** -0.5),
        'ssm_c_im': nrm(ks[13], (SSM_GROUPS, SSM_GROUP_SIZE, SSM_STATE), SSM_STATE ** -0.5),
        'ssm_d': nrm(ks[14], (SSM_WIDTH,), 0.5),
        'ssm_w_glu': nrm(ks[15], (SSM_WIDTH, SSM_WIDTH), SSM_WIDTH ** -0.5),
        'w_branch_ssm': nrm(ks[16], (SSM_WIDTH, D_MODEL), SSM_WIDTH ** -0.5),
        'w_branch_attn': nrm(ks[17], (ATTN_OUT_WIDTH, D_MODEL), ATTN_OUT_WIDTH ** -0.5),
        'w_out': nrm(ks[18], (D_MODEL, D_MODEL), D_MODEL ** -0.5 * DEEPNORM_BETA),
        'ln1_g': 1.0 + nrm(ks[19], (D_MODEL,), 0.01),
        'ln1_b': nrm(ks[20], (D_MODEL,), 0.01),
        'peer_w_q': nrm(ks[21], (D_MODEL, PEER_HEADS * PEER_QUERY_DIM), D_MODEL ** -0.5),
        'peer_sub_keys_1': nrm(ks[22], (PEER_N_KEYS, PEER_HALF_DIM), PEER_HALF_DIM ** -0.5),
        'peer_sub_keys_2': nrm(ks[23], (PEER_N_KEYS, PEER_HALF_DIM), PEER_HALF_DIM ** -0.5),
        'peer_u': nrm(ks[24], (PEER_EXPERTS, D_MODEL), D_MODEL ** -0.5),
        'peer_v': nrm(ks[25], (PEER_EXPERTS, D_MODEL), DEEPNORM_BETA),
        'ln2_g': 1.0 + nrm(ks[26], (D_MODEL,), 0.01),
        'ln2_b': nrm(ks[27], (D_MODEL,), 0.01),
    }


def reference(x_prompt, x_sample, cache_kv_w128, cache_kv_w512, cache_kv_w2048, state_ssm,
              w_in, ssm_log_dt, ssm_a_re, ssm_a_im, ssm_b_re, ssm_b_im, ssm_c_re, ssm_c_im,
              ssm_d, ssm_w_glu, w_branch_ssm, w_branch_attn, w_out, ln1_g, ln1_b,
              peer_w_q, peer_sub_keys_1, peer_sub_keys_2, peer_u, peer_v, ln2_g, ln2_b):
    y_prompt, y_sample = x_prompt, x_sample
    for _layer in range(DEPTH):
        y_prompt, kv_p, ssm_p = _decoder_layer(
            y_prompt, None, None, w_in, ssm_log_dt, ssm_a_re, ssm_a_im, ssm_b_re, ssm_b_im,
            ssm_c_re, ssm_c_im, ssm_d, ssm_w_glu, w_branch_ssm, w_branch_attn, w_out,
            ln1_g, ln1_b, peer_w_q, peer_sub_keys_1, peer_sub_keys_2, peer_u, peer_v, ln2_g, ln2_b)
        y_sample, kv_s, ssm_s = _decoder_layer(
            y_sample, (cache_kv_w128, cache_kv_w512, cache_kv_w2048), state_ssm,
            w_in, ssm_log_dt, ssm_a_re, ssm_a_im, ssm_b_re, ssm_b_im,
            ssm_c_re, ssm_c_im, ssm_d, ssm_w_glu, w_branch_ssm, w_branch_attn, w_out,
            ln1_g, ln1_b, peer_w_q, peer_sub_keys_1, peer_sub_keys_2, peer_u, peer_v, ln2_g, ln2_b)
    return (y_prompt, y_sample, kv_p[0], kv_p[1], kv_p[2], ssm_p, kv_s[0], kv_s[1], kv_s[2], ssm_s)
```

```python
import functools
import math

import jax
import jax.numpy as jnp
from jax import lax
from jax.experimental import pallas as pl
from jax.experimental.pallas import tpu as pltpu

F32 = jnp.float32
BF16 = jnp.bfloat16
I32 = jnp.int32

HEAD_DIM = 128
HEADS_PER_GROUP = 4
DILATIONS = (1, 4, 16)
PAST_LEN = 16384
PEER_HEADS = 8
PEER_TOPK = 16
DEPTH = 1
DEEPNORM_ALPHA = (2.0 * DEPTH) ** 0.25
LN_EPS = 1e-5
SSM_CHUNK = 8
SSM_PACK = 8
NEG = -1e30
VMEM_LIMIT = 56 * 1024 * 1024

NT_DIMS = (((1,), (1,)), ((), ()))


def _cparams(sem):
    return pltpu.CompilerParams(dimension_semantics=sem, vmem_limit_bytes=VMEM_LIMIT)


def _gelu_tanh(x):
    return 0.5 * x * (1.0 + jnp.tanh(math.sqrt(2.0 / math.pi) * (x + 0.044715 * (x * x * x))))


def _sigmoid(x):
    return 1.0 / (1.0 + jnp.exp(-x))


def _layer_norm(x, g, b):
    mu = jnp.mean(x, axis=-1, keepdims=True)
    xc = x - mu
    var = jnp.mean(xc * xc, axis=-1, keepdims=True)
    return xc * lax.rsqrt(var + LN_EPS) * g + b


def _pick_tile(n, pref):
    t = min(n, pref)
    while n % t:
        t //= 2
    return t


def _proj_kernel(x_ref, w_ref, o_ref):
    o_ref[...] = jnp.dot(x_ref[...].astype(BF16), w_ref[...], preferred_element_type=F32)


def _proj(x2d, w_bf16, tm, tn):
    n, d = x2d.shape
    nw = w_bf16.shape[1]
    return pl.pallas_call(
        _proj_kernel,
        grid=(n // tm, nw // tn),
        in_specs=[pl.BlockSpec((tm, d), lambda i, j: (i, 0)),
                  pl.BlockSpec((d, tn), lambda i, j: (0, j))],
        out_specs=pl.BlockSpec((tm, tn), lambda i, j: (i, j)),
        out_shape=jax.ShapeDtypeStruct((n, nw), F32),
        compiler_params=_cparams(("parallel", "arbitrary")),
        name="in_proj",
    )(x2d, w_bf16)


def _rows(start, size, stride):
    return pl.ds(start, size) if stride == 1 else pl.ds(start, size, stride=stride)


def _attn_prompt_kernel(slopes_ref, q_ref, k_ref, v_ref, o_ref, m_ref, l_ref, *, seq, qb, scale):
    hs = pl.program_id(1)
    g = pl.program_id(2)
    n_groups = len(DILATIONS)
    for gi, d in enumerate(DILATIONS):
        @pl.when(g == gi)
        def _(gi=gi, d=d):
            slope = slopes_ref[gi * HEADS_PER_GROUP + hs] * float(d)
            for r in range(d):
                for i in range(seq // (d * qb)):
                    rq = _rows(r + i * qb * d, qb, d)
                    q = q_ref[0, rq, :].astype(BF16)
                    if i == 0:
                        rk, nk, koff = rq, qb, 0
                    else:
                        rk, nk, koff = _rows(r + (i - 1) * qb * d, 2 * qb, d), 2 * qb, qb
                    kk = k_ref[0, rk, :].astype(BF16)
                    vv = v_ref[0, rk, :].astype(BF16)
                    s = lax.dot_general(q, kk, NT_DIMS, preferred_element_type=F32) * scale
                    row = lax.broadcasted_iota(I32, (qb, nk), 0)
                    col = lax.broadcasted_iota(I32, (qb, nk), 1)
                    j = row + koff - col
                    s = s - slope * j.astype(F32)
                    s = jnp.where((j >= 0) & (j <= qb), s, NEG)
                    m_blk = jnp.max(s, axis=1, keepdims=True)
                    p = jnp.exp(s - m_blk)
                    l_blk = jnp.sum(p, axis=1, keepdims=True)
                    o_blk = jnp.dot(p.astype(BF16), vv, preferred_element_type=F32)
                    m_b = jnp.broadcast_to(m_blk, (qb, HEAD_DIM))
                    l_b = jnp.broadcast_to(l_blk, (qb, HEAD_DIM))
                    if gi == 0:
                        m_new, l_new, acc = m_b, l_b, o_blk
                    else:
                        m_old = m_ref[rq, :]
                        m_new = jnp.maximum(m_old, m_b)
                        a_old = jnp.exp(m_old - m_new)
                        a_blk = jnp.exp(m_b - m_new)
                        l_new = a_old * l_ref[rq, :] + a_blk * l_b
                        acc = a_old * o_ref[0, rq, :] + a_blk * o_blk
                    if gi == n_groups - 1:
                        o_ref[0, rq, :] = acc / l_new
                    else:
                        m_ref[rq, :] = m_new
                        l_ref[rq, :] = l_new
                        o_ref[0, rq, :] = acc


def _attn_prompt(z3, slopes, ssm_width, qkv_width, qb):
    bsz, seq, _ = z3.shape
    blk = lambda base: (lambda b, hs, g: (b, 0, base + g * HEADS_PER_GROUP + hs))
    qc, kc, vc = (ssm_width // HEAD_DIM, (ssm_width + qkv_width) // HEAD_DIM,
                  (ssm_width + 2 * qkv_width) // HEAD_DIM)
    return pl.pallas_call(
        functools.partial(_attn_prompt_kernel, seq=seq, qb=qb, scale=HEAD_DIM ** -0.5),
        grid=(bsz, HEADS_PER_GROUP, len(DILATIONS)),
        in_specs=[pl.BlockSpec(memory_space=pltpu.SMEM),
                  pl.BlockSpec((1, seq, HEAD_DIM), blk(qc)),
                  pl.BlockSpec((1, seq, HEAD_DIM), blk(kc)),
                  pl.BlockSpec((1, seq, HEAD_DIM), blk(vc))],
        out_specs=pl.BlockSpec((1, seq, HEAD_DIM), lambda b, hs, g: (b, 0, hs)),
        out_shape=jax.ShapeDtypeStruct((bsz, seq, HEADS_PER_GROUP * HEAD_DIM), F32),
        scratch_shapes=[pltpu.VMEM((seq, HEAD_DIM), F32), pltpu.VMEM((seq, HEAD_DIM), F32)],
        compiler_params=_cparams(("parallel", "parallel", "arbitrary")),
        name="attn_prompt",
    )(slopes, z3, z3, z3)


def _attn_sample_kernel(slopes_ref, q_ref, k_ref, v_ref, c0_ref, c1_ref, c2_ref, o_ref, *, n_new, scale):
    caches = (c0_ref, c1_ref, c2_ref)
    kv_half = HEADS_PER_GROUP * HEAD_DIM
    for hs in range(HEADS_PER_GROUP):
        pieces = []
        for gi, d in enumerate(DILATIONS):
            c_ref = caches[gi]
            win = c_ref.shape[1]
            hcol = (gi * HEADS_PER_GROUP + hs) * HEAD_DIM
            slope = slopes_ref[gi * HEADS_PER_GROUP + hs]
            q = q_ref[0, :, hcol:hcol + HEAD_DIM].astype(BF16)
            k_new = k_ref[0, :, hcol:hcol + HEAD_DIM].astype(BF16)
            v_new = v_ref[0, :, hcol:hcol + HEAD_DIM].astype(BF16)
            k_old = c_ref[0, :, hs * HEAD_DIM:(hs + 1) * HEAD_DIM].astype(BF16)
            v_old = c_ref[0, :, kv_half + hs * HEAD_DIM:kv_half + (hs + 1) * HEAD_DIM].astype(BF16)
            for kk, vv, nk, base in ((k_old, v_old, win, win), (k_new, v_new, n_new, 0)):
                s = lax.dot_general(q, kk, NT_DIMS, preferred_element_type=F32) * scale
                row = lax.broadcasted_iota(I32, (n_new, nk), 0)
                col = lax.broadcasted_iota(I32, (n_new, nk), 1)
                offs = base + row - col
                valid = (offs >= 0) & (offs <= win) & ((offs & (d - 1)) == 0)
                s = jnp.where(valid, s - slope * offs.astype(F32), NEG)
                pieces.append((s, vv))
        m = pieces[0][0].max(axis=1, keepdims=True)
        for s, _ in pieces[1:]:
            m = jnp.maximum(m, s.max(axis=1, keepdims=True))
        l = jnp.zeros((n_new, 1), F32)
        acc = jnp.zeros((n_new, HEAD_DIM), F32)
        for s, vv in pieces:
            p = jnp.exp(s - m)
            l = l + p.sum(axis=1, keepdims=True)
            acc = acc + jnp.dot(p.astype(BF16), vv, preferred_element_type=F32)
        o_ref[0, :, hs * HEAD_DIM:(hs + 1) * HEAD_DIM] = acc / l


def _attn_sample(slopes, q, k, v, caches):
    bsz, n_new, qkv_width = q.shape
    assert all(d & (d - 1) == 0 for d in DILATIONS)
    row_spec = lambda a: pl.BlockSpec((1,) + a.shape[1:], lambda b: (b, 0, 0))
    return pl.pallas_call(
        functools.partial(_attn_sample_kernel, n_new=n_new, scale=HEAD_DIM ** -0.5),
        grid=(bsz,),
        in_specs=[pl.BlockSpec(memory_space=pltpu.SMEM), row_spec(q), row_spec(k), row_spec(v)]
                 + [row_spec(c) for c in caches],
        out_specs=pl.BlockSpec((1, n_new, HEADS_PER_GROUP * HEAD_DIM), lambda b: (b, 0, 0)),
        out_shape=jax.ShapeDtypeStruct((bsz, n_new, HEADS_PER_GROUP * HEAD_DIM), F32),
        compiler_params=_cparams(("parallel",)),
        name="attn_sample",
    )(slopes, q, k, v, *caches)


def _ssm_prep_kernel(ldt_ref, ar_r_ref, ai_r_ref, ar_c_ref, ai_c_ref, br_ref, bi_ref, cr_ref, ci_ref,
                     kt_ref, e_ref, f_ref, at_ref, *, chunk):
    dt = jnp.exp(ldt_ref[0])

    def discretise(ar, ai):
        mag = jnp.exp(ar * dt)
        return mag * jnp.cos(ai * dt), mag * jnp.sin(ai * dt)

    ar_r, ai_r = ar_r_ref[0], ai_r_ref[0]
    ar_c, ai_c = ar_c_ref[0], ai_c_ref[0]
    abr_r, abi_r = discretise(ar_r, ai_r)
    abr_c, abi_c = discretise(ar_c, ai_c)
    den = ar_c * ar_c + ai_c * ai_c
    f_re = ((abr_c - 1.0) * ar_c + abi_c * ai_c) / den
    f_im = (abi_c * ar_c - (abr_c - 1.0) * ai_c) / den
    b_re, b_im = br_ref[0], bi_ref[0]
    bb_re = f_re * b_re - f_im * b_im
    bb_im = f_re * b_im + f_im * b_re
    c_re, c_im = cr_ref[0], ci_ref[0]

    def powers(re1, im1):
        pr, pi = [jnp.ones_like(re1)], [jnp.zeros_like(re1)]
        for _ in range(chunk):
            pr.append(pr[-1] * re1 - pi[-1] * im1)
            pi.append(pr[-2] * im1 + pi[-1] * re1)
        return pr, pi

    pr_r, pi_r = powers(abr_r, abi_r)
    pr_c, pi_c = powers(abr_c, abi_c)
    hi = lax.Precision.HIGHEST
    for t in range(chunk):
        cp_re = c_re * pr_r[t] - c_im * pi_r[t]
        cp_im = c_re * pi_r[t] + c_im * pr_r[t]
        kt_ref[0, t] = (jnp.dot(cp_re, bb_re, precision=hi, preferred_element_type=F32)
                        - jnp.dot(cp_im, bb_im, precision=hi, preferred_element_type=F32))
        qr, qi = pr_c[chunk - 1 - t], pi_c[chunk - 1 - t]
        e_ref[0, t, 0] = qr * bb_re - qi * bb_im
        e_ref[0, t, 1] = qr * bb_im + qi * bb_re
        f_ref[0, t, 0] = c_re * pr_r[t + 1] - c_im * pi_r[t + 1]
        f_ref[0, t, 1] = -(c_re * pi_r[t + 1] + c_im * pr_r[t + 1])
    at_ref[0, 0:1, :] = pr_r[chunk]
    at_ref[0, 1:2, :] = pi_r[chunk]


def _ssm_prep(log_dt, a_re, a_im, b_re, b_im, c_re, c_im, chunk):
    g, p = a_re.shape
    h = b_re.shape[2]
    g3 = lambda *s: pl.BlockSpec((1,) + s, lambda i: (i,) + (0,) * len(s))
    return pl.pallas_call(
        functools.partial(_ssm_prep_kernel, chunk=chunk),
        grid=(g,),
        in_specs=[g3(1, 1), g3(1, p), g3(1, p), g3(p, 1), g3(p, 1), g3(p, h), g3(p, h), g3(h, p), g3(h, p)],
        out_specs=[g3(chunk, h, h), g3(chunk, 2, p, h), g3(chunk, 2, h, p), g3(2, p)],
        out_shape=[jax.ShapeDtypeStruct((g, chunk, h, h), F32),
                   jax.ShapeDtypeStruct((g, chunk, 2, p, h), F32),
                   jax.ShapeDtypeStruct((g, chunk, 2, h, p), F32),
                   jax.ShapeDtypeStruct((g, 2, p), F32)],
        compiler_params=_cparams(("parallel",)),
        name="ssm_prep",
    )(log_dt.reshape(g, 1, 1), a_re.reshape(g, 1, p), a_im.reshape(g, 1, p),
      a_re.reshape(g, p, 1), a_im.reshape(g, p, 1), b_re, b_im, c_re, c_im)


def _ssm_layout(kt, e, f, at, chunk):
    g, _, h, _ = kt.shape
    p = at.shape[2]
    ns = g // SSM_PACK
    eye = jnp.eye(SSM_PACK, dtype=F32)
    lag = jnp.arange(chunk)[None, :] - jnp.arange(chunk)[:, None]
    ktoe = jnp.where((lag >= 0)[None, :, :, None, None],
                     kt[:, jnp.clip(lag, 0, chunk - 1)], 0.0)
    ktoe = ktoe.reshape(ns, SSM_PACK, chunk, chunk, h, h)
    m = jnp.einsum('ogijkh,gq->oighjqk', ktoe, eye).reshape(ns, chunk * SSM_PACK * h, chunk * SSM_PACK * h)
    e6 = e.reshape(ns, SSM_PACK, chunk, 2, p, h)
    em = jnp.einsum('ogirph,gq->oighrqp', e6, eye).reshape(ns, chunk * SSM_PACK * h, 2 * SSM_PACK * p)
    f6 = f.reshape(ns, SSM_PACK, chunk, 2, h, p)
    fm = jnp.einsum('ogirhp,gq->orgpiqh', f6, eye).reshape(ns, 2 * SSM_PACK * p, chunk * SSM_PACK * h)
    am = at.reshape(ns, SSM_PACK, 2, p).transpose(0, 2, 1, 3).reshape(ns, 2, SSM_PACK * p)
    return m.astype(BF16), em, fm.astype(BF16), am


def _ssm_kernel(u_ref, m_ref, e_ref, f_ref, a_ref, h0_ref, y_ref, hfin_ref, h_scr, s_scr, hp_scr,
                *, bt, n_chunks, exact_state):
    half = a_ref.shape[2]

    @pl.when(pl.program_id(1) == 0)
    def _():
        h_scr[...] = h0_ref[0]

    u = u_ref[...]
    if exact_state:
        s_scr[...] = jnp.dot(u, e_ref[0], precision=lax.Precision.HIGHEST, preferred_element_type=F32)
    else:
        s_scr[...] = jnp.dot(u.astype(BF16), e_ref[0].astype(BF16), preferred_element_type=F32)
    a_re = jnp.broadcast_to(a_ref[0, 0:1, :], (bt, half))
    a_im = jnp.broadcast_to(a_ref[0, 1:2, :], (bt, half))

    def step(c, carry):
        h_re, h_im = carry
        rows = pl.ds(pl.multiple_of(c * bt, bt), bt)
        hp_scr[rows, 0:half] = h_re
        hp_scr[rows, half:2 * half] = h_im
        n_re = a_re * h_re - a_im * h_im + s_scr[rows, 0:half]
        n_im = a_re * h_im + a_im * h_re + s_scr[rows, half:2 * half]
        return n_re, n_im

    h_re, h_im = lax.fori_loop(0, n_chunks, step, (h_scr[:, 0:half], h_scr[:, half:2 * half]))
    h_scr[:, 0:half] = h_re
    h_scr[:, half:2 * half] = h_im
    hfin_ref[0] = h_scr[...]
    y_ref[...] = (jnp.dot(u.astype(BF16), m_ref[0], preferred_element_type=F32)
                  + jnp.dot(hp_scr[...].astype(BF16), f_ref[0], preferred_element_type=F32))


def _ssm(u2, m, em, fm, am, h0, bt, exact_state):
    rows, _ = u2.shape
    ns, sw, _ = m.shape
    st = em.shape[2]
    tr = _pick_tile(rows // bt, 64) * bt
    slab = lambda i, r: (i, 0, 0)
    return pl.pallas_call(
        functools.partial(_ssm_kernel, bt=bt, n_chunks=tr // bt, exact_state=exact_state),
        grid=(ns, rows // tr),
        in_specs=[pl.BlockSpec((tr, sw), lambda i, r: (r, i)),
                  pl.BlockSpec((1, sw, sw), slab), pl.BlockSpec((1, sw, st), slab),
                  pl.BlockSpec((1, st, sw), slab), pl.BlockSpec((1, 2, st // 2), slab),
                  pl.BlockSpec((1, bt, st), slab)],
        out_specs=[pl.BlockSpec((tr, sw), lambda i, r: (r, i)), pl.BlockSpec((1, bt, st), slab)],
        out_shape=[jax.ShapeDtypeStruct((rows, ns * sw), F32), jax.ShapeDtypeStruct((ns, bt, st), F32)],
        scratch_shapes=[pltpu.VMEM((bt, st), F32), pltpu.VMEM((tr, st), F32), pltpu.VMEM((tr, st), F32)],
        compiler_params=_cparams(("parallel", "arbitrary")),
        name="ssm_scan",
    )(u2, m, em, fm, am, h0)


def _to_chunk_rows(u, chunk, lanes):
    b, l, w = u.shape
    c = l // chunk
    return u.reshape(b, c, chunk, w // lanes, lanes).transpose(1, 0, 3, 2, 4).reshape(c * b, chunk * w)


def _from_chunk_rows(y2, b, chunk, lanes):
    rows, cw = y2.shape
    c, w = rows // b, cw // chunk
    return y2.reshape(c, b, w // lanes, chunk, lanes).transpose(1, 0, 3, 2, 4).reshape(b * c * chunk, w)


def _state_to_slabs(h, ns):
    b, g, p, _ = h.shape
    return h.reshape(b, ns, SSM_PACK, p, 2).transpose(1, 0, 4, 2, 3).reshape(ns, b, 2 * SSM_PACK * p)


def _state_from_slabs(hs, p):
    ns, b, _ = hs.shape
    return hs.reshape(ns, b, 2, SSM_PACK, p).transpose(1, 0, 3, 4, 2).reshape(b, ns * SSM_PACK, p, 2)


def _merge_kernel(yc_ref, u_ref, gs_ref, ga_ref, at_ref, x_ref, d_ref, wglu_ref, wbs_ref, wba_ref, wout_ref,
                  g1_ref, b1_ref, h_ref):
    y = yc_ref[...] + d_ref[...] * u_ref[...]
    gl = _gelu_tanh(y)
    so = gl * _sigmoid(jnp.dot(gl.astype(BF16), wglu_ref[...], preferred_element_type=F32))
    merged = (_sigmoid(gs_ref[...]) * jnp.dot(so.astype(BF16), wbs_ref[...], preferred_element_type=F32)
              + _sigmoid(ga_ref[...]) * jnp.dot(at_ref[...].astype(BF16), wba_ref[...], preferred_element_type=F32))
    mix = jnp.dot(merged.astype(BF16), wout_ref[...], preferred_element_type=F32)
    h_ref[...] = _layer_norm(DEEPNORM_ALPHA * x_ref[...] + mix, g1_ref[...], b1_ref[...])


def _merge(yc, z, gates, attn, x2d, d_skip, w_glu, w_bs, w_ba, w_out, ln_g, ln_b, tm):
    n, dm = x2d.shape
    aw = attn.shape[1]
    ssm_width = yc.shape[1]
    row = lambda w, cb=0: pl.BlockSpec((tm, w), lambda i: (i, cb))
    full = lambda a: pl.BlockSpec(a.shape, lambda i: (0, 0), pipeline_mode=pl.Buffered(1))
    return pl.pallas_call(
        _merge_kernel,
        grid=(n // tm,),
        in_specs=[row(ssm_width), row(ssm_width), row(dm, 0), row(dm, 1), row(aw), row(dm),
                  full(d_skip), full(w_glu), full(w_bs), full(w_ba), full(w_out), full(ln_g), full(ln_b)],
        out_specs=row(dm),
        out_shape=jax.ShapeDtypeStruct((n, dm), F32),
        compiler_params=_cparams(("parallel",)),
        name="merge",
    )(yc, z, gates, gates, attn, x2d, d_skip, w_glu, w_bs, w_ba, w_out, ln_g, ln_b)


def _top_rows(s, k, payload=None):
    n = s.shape[0]
    iota = lax.broadcasted_iota(I32, s.shape, 0)
    vals, idxs, pays = [], [], []
    for _ in range(k):
        m = jnp.max(s, axis=0, keepdims=True)
        idx = jnp.min(jnp.where(s == m, iota, n), axis=0, keepdims=True)
        hit = iota == idx
        vals.append(m)
        idxs.append(idx)
        if payload is not None:
            pays.append(jnp.max(jnp.where(hit, payload, -1), axis=0, keepdims=True))
        s = jnp.where(hit, NEG, s)
    return vals, idxs, pays


def _peer_topk_kernel(h_ref, wq_ref, k1_ref, k2_ref, a_ref, b_ref, g_ref, *, half, n_keys):
    q = jnp.dot(h_ref[...].astype(BF16), wq_ref[...], preferred_element_type=F32)
    hi = lax.Precision.HIGHEST
    shift = n_keys.bit_length() - 1
    e_rows, g_rows = [], []
    for hd in range(PEER_HEADS):
        q1 = q[:, hd * 2 * half:hd * 2 * half + half]
        q2 = q[:, hd * 2 * half + half:(hd + 1) * 2 * half]
        s1 = lax.dot_general(k1_ref[...], q1, NT_DIMS, precision=hi, preferred_element_type=F32)
        s2 = lax.dot_general(k2_ref[...], q2, NT_DIMS, precision=hi, preferred_element_type=F32)
        v1, i1, _ = _top_rows(s1, PEER_TOPK)
        v2, i2, _ = _top_rows(s2, PEER_TOPK)
        v2a = jnp.concatenate(v2, axis=0)
        i2a = jnp.concatenate(i2, axis=0)
        cand = jnp.concatenate([v + v2a for v in v1], axis=0)
        cidx = jnp.concatenate([i * n_keys + i2a for i in i1], axis=0)
        top, _, eidx = _top_rows(cand, PEER_TOPK, payload=cidx)
        ex = [jnp.exp(t - top[0]) for t in top]
        tot = ex[0]
        for x in ex[1:]:
            tot = tot + x
        e_rows += eidx
        g_rows += [x / tot for x in ex]
    e_all = jnp.concatenate(e_rows, axis=0)
    g_all = jnp.concatenate(g_rows, axis=0)
    a_ref[...] = (e_all >> shift).astype(F32).T.astype(I32)
    b_ref[...] = (e_all & (n_keys - 1)).astype(F32).T.astype(I32)
    g_ref[...] = g_all.T


def _peer_topk(h, wq_bf16, k1, k2, tm):
    n, dm = h.shape
    n_keys, half = k1.shape
    ne = PEER_HEADS * PEER_TOPK
    assert n_keys & (n_keys - 1) == 0
    full = lambda a: pl.BlockSpec(a.shape, lambda i: (0, 0))
    out = lambda: pl.BlockSpec((tm, ne), lambda i: (i, 0))
    return pl.pallas_call(
        functools.partial(_peer_topk_kernel, half=half, n_keys=n_keys),
        grid=(n // tm,),
        in_specs=[pl.BlockSpec((tm, dm), lambda i: (i, 0)), full(wq_bf16), full(k1), full(k2)],
        out_specs=[out(), out(), out()],
        out_shape=[jax.ShapeDtypeStruct((n, ne), I32), jax.ShapeDtypeStruct((n, ne), I32),
                   jax.ShapeDtypeStruct((n, ne), F32)],
        compiler_params=_cparams(("parallel",)),
        name="peer_topk",
    )(h, wq_bf16, k1, k2)


def _peer_dense_kernel(a_ref, b_ref, g_ref, h_ref, u_ref, v_ref, g2_ref, b2_ref, y_ref, w_scr, x_scr, acc_scr,
                       *, tm, n_keys, rows_per_step):
    e_blk = pl.program_id(1)
    ne = a_ref.shape[1]

    @pl.when(e_blk == 0)
    def _():
        x_scr[...] = h_ref[...].astype(BF16)
        acc_scr[...] = jnp.zeros_like(acc_scr)
        key_iota = lax.broadcasted_iota(I32, (n_keys, ne), 0)

        def build(t, carry):
            a_row = a_ref[pl.ds(t, 1), :]
            b_row = b_ref[pl.ds(t, 1), :]
            g_row = g_ref[pl.ds(t, 1), :]
            g_hi = g_row.astype(BF16).astype(F32)
            g_lo = g_row - g_hi
            sel_a = jnp.where(key_iota == a_row, 1.0, 0.0).astype(BF16)
            hit_b = key_iota == b_row
            lhs = jnp.concatenate([sel_a, sel_a], axis=1)
            rhs = jnp.concatenate([jnp.where(hit_b, g_hi, 0.0).astype(BF16),
                                   jnp.where(hit_b, g_lo, 0.0).astype(BF16)], axis=1)
            w_scr[pl.ds(pl.multiple_of(t * n_keys, n_keys), n_keys), :] = lax.dot_general(
                lhs, rhs, NT_DIMS, preferred_element_type=F32)
            return carry

        lax.fori_loop(0, tm, build, 0)

    z = lax.dot_general(x_scr[...], u_ref[...], NT_DIMS, preferred_element_type=F32)
    parts = []
    for c in range(rows_per_step):
        w_c = w_scr[pl.ds(e_blk * rows_per_step + c, tm, stride=n_keys), :]
        parts.append((w_c * _gelu_tanh(z[:, c * n_keys:(c + 1) * n_keys])).astype(BF16))
    acc_scr[...] += jnp.dot(jnp.concatenate(parts, axis=1), v_ref[...], preferred_element_type=F32)

    @pl.when(e_blk == pl.num_programs(1) - 1)
    def _():
        y_ref[...] = _layer_norm(DEEPNORM_ALPHA * h_ref[...] + acc_scr[...], g2_ref[...], b2_ref[...])


def _peer_dense(a, b, g, h, u_bf16, v_bf16, ln_g, ln_b, n_keys, tm, rows_per_step):
    n, dm = h.shape
    ne = a.shape[1]
    eb = rows_per_step * n_keys
    tok = lambda w: pl.BlockSpec((tm, w), lambda i, e: (i, 0))
    full = lambda arr: pl.BlockSpec(arr.shape, lambda i, e: (0, 0))
    return pl.pallas_call(
        functools.partial(_peer_dense_kernel, tm=tm, n_keys=n_keys, rows_per_step=rows_per_step),
        grid=(n // tm, n_keys // rows_per_step),
        in_specs=[tok(ne), tok(ne), tok(ne), tok(dm),
                  pl.BlockSpec((eb, dm), lambda i, e: (e, 0)), pl.BlockSpec((eb, dm), lambda i, e: (e, 0)),
                  full(ln_g), full(ln_b)],
        out_specs=tok(dm),
        out_shape=jax.ShapeDtypeStruct((n, dm), F32),
        scratch_shapes=[pltpu.VMEM((tm * n_keys, n_keys), F32), pltpu.VMEM((tm, dm), BF16),
                        pltpu.VMEM((tm, dm), F32)],
        compiler_params=_cparams(("parallel", "arbitrary")),
        name="peer_experts",
    )(a, b, g, h, u_bf16, v_bf16, ln_g, ln_b)


def _layer(x, caches, h0, wts, ssm_mats):
    (w_in, ssm_d, w_glu, w_bs, w_ba, w_out, ln1_g, ln1_b, wq, k1, k2, pu, pv, ln2_g, ln2_b, slopes) = wts
    m, em, fm, am = ssm_mats
    bsz, seq, dm = x.shape
    n = bsz * seq
    ssm_width = ssm_d.shape[1]
    in_width = w_in.shape[1]
    qkv_width = (in_width - ssm_width - 2 * dm) // 3
    kv_half = HEADS_PER_GROUP * HEAD_DIM
    k_off, v_off = ssm_width + qkv_width, ssm_width + 2 * qkv_width
    x2d = x.reshape(n, dm)

    gate_off = ssm_width + 3 * qkv_width
    z = _proj(x2d, w_in[:, :gate_off], _pick_tile(n, 1024), _pick_tile(gate_off, 512))
    gates = _proj(x2d, w_in[:, gate_off:], _pick_tile(n, 1024), _pick_tile(2 * dm, 512))
    z3 = z.reshape(bsz, seq, gate_off)

    new_kv = []
    if caches is None:
        qb = seq // max(DILATIONS)
        attn = _attn_prompt(z3, slopes, ssm_width, qkv_width, qb).reshape(n, kv_half)
        for gi, d in enumerate(DILATIONS):
            keep = min(qb * d, seq)
            kg = z3[:, seq - keep:, k_off + gi * kv_half:k_off + (gi + 1) * kv_half]
            vg = z3[:, seq - keep:, v_off + gi * kv_half:v_off + (gi + 1) * kv_half]
            new_kv.append(jnp.stack([kg, vg], axis=2).reshape(bsz, keep, 2, HEADS_PER_GROUP, HEAD_DIM))
    else:
        q = z3[:, :, ssm_width:k_off]
        k = z3[:, :, k_off:v_off]
        v = z3[:, :, v_off:v_off + qkv_width]
        flat = [c.reshape(c.shape[0], c.shape[1], 2 * kv_half) for c in caches]
        attn = _attn_sample(slopes, q, k, v, flat).reshape(n, kv_half)
        for gi in range(len(DILATIONS)):
            kg = k[:, :, gi * kv_half:(gi + 1) * kv_half]
            vg = v[:, :, gi * kv_half:(gi + 1) * kv_half]
            new_kv.append(jnp.stack([kg, vg], axis=2).reshape(bsz, seq, 2, HEADS_PER_GROUP, HEAD_DIM))

    ns = m.shape[0]
    p_state = em.shape[2] // (2 * SSM_PACK)
    lanes = ssm_width // ns
    u3 = z3[:, :, :ssm_width]
    exact_state = h0 is not None
    u2 = _to_chunk_rows(u3 if exact_state else u3.astype(BF16), SSM_CHUNK, lanes)
    h0s = (_state_to_slabs(h0, ns) if exact_state
           else jnp.zeros((ns, bsz, 2 * SSM_PACK * p_state), F32))
    y2, hfin = _ssm(u2, m, em, fm, am, h0s, bsz, exact_state)
    yc = _from_chunk_rows(y2, bsz, SSM_CHUNK, lanes)
    ssm_state = _state_from_slabs(hfin, p_state)

    tm = _pick_tile(n, 256)
    h = _merge(yc, z, gates, attn, x2d, ssm_d, w_glu, w_bs, w_ba, w_out, ln1_g, ln1_b, tm)
    a, b, g = _peer_topk(h, wq, k1, k2, tm)
    y = _peer_dense(a, b, g, h, pu, pv, ln2_g, ln2_b, k1.shape[0], tm, 4)
    return y.reshape(bsz, seq, dm), new_kv, ssm_state


def kernel(x_prompt, x_sample, cache_kv_w128, cache_kv_w512, cache_kv_w2048, state_ssm, w_in, ssm_log_dt, ssm_a_re, ssm_a_im, ssm_b_re, ssm_b_im, ssm_c_re, ssm_c_im, ssm_d, ssm_w_glu, w_branch_ssm, w_branch_attn, w_out, ln1_g, ln1_b, peer_w_q, peer_sub_keys_1, peer_sub_keys_2, peer_u, peer_v, ln2_g, ln2_b):
    caches = (cache_kv_w128, cache_kv_w512, cache_kv_w2048)
    seq = x_prompt.shape[1]
    assert x_sample.shape[1] == SSM_CHUNK and seq % (SSM_CHUNK * max(DILATIONS)) == 0
    assert all(c.shape[1] == (seq // max(DILATIONS)) * d and c.shape[1] <= PAST_LEN
               for c, d in zip(caches, DILATIONS))
    n_heads = len(DILATIONS) * HEADS_PER_GROUP
    slopes = 2.0 ** (-8.0 * jnp.arange(1, n_heads + 1, dtype=F32) / n_heads)
    row = lambda a: a.reshape(1, -1).astype(F32)
    wts = (w_in.astype(BF16), row(ssm_d), ssm_w_glu.astype(BF16), w_branch_ssm.astype(BF16),
           w_branch_attn.astype(BF16), w_out.astype(BF16), row(ln1_g), row(ln1_b), peer_w_q.astype(BF16),
           peer_sub_keys_1.astype(F32), peer_sub_keys_2.astype(F32), peer_u.astype(BF16), peer_v.astype(BF16),
           row(ln2_g), row(ln2_b), slopes)
    ssm_mats = _ssm_layout(*_ssm_prep(ssm_log_dt, ssm_a_re, ssm_a_im, ssm_b_re, ssm_b_im, ssm_c_re, ssm_c_im,
                                      SSM_CHUNK), SSM_CHUNK)
    y_p, kv_p, ssm_p = _layer(x_prompt, None, None, wts, ssm_mats)
    y_s, kv_s, ssm_s = _layer(x_sample, caches, state_ssm, wts, ssm_mats)
    return (y_p, y_s, kv_p[0], kv_p[1], kv_p[2], ssm_p, kv_s[0], kv_s[1], kv_s[2], ssm_s)
```

```python
import functools
import math

import jax
import jax.numpy as jnp
from jax import lax
from jax.experimental import pallas as pl
from jax.experimental.pallas import tpu as pltpu

F32 = jnp.float32
BF16 = jnp.bfloat16
I32 = jnp.int32

HEAD_DIM = 128
HEADS_PER_GROUP = 4
DILATIONS = (1, 4, 16)
PAST_LEN = 16384
PEER_HEADS = 8
PEER_TOPK = 16
DEPTH = 1
DEEPNORM_ALPHA = (2.0 * DEPTH) ** 0.25
LN_EPS = 1e-5
SSM_CHUNK = 8
SSM_PACK = 8
NEG = -1e30
VMEM_LIMIT = 56 * 1024 * 1024

NT_DIMS = (((1,), (1,)), ((), ()))


def _cparams(sem):
    return pltpu.CompilerParams(dimension_semantics=sem, vmem_limit_bytes=VMEM_LIMIT)


def _gelu_tanh(x):
    return 0.5 * x * (1.0 + jnp.tanh(math.sqrt(2.0 / math.pi) * (x + 0.044715 * (x * x * x))))


def _sigmoid(x):
    return 1.0 / (1.0 + jnp.exp(-x))


def _layer_norm(x, g, b):
    mu = jnp.mean(x, axis=-1, keepdims=True)
    xc = x - mu
    var = jnp.mean(xc * xc, axis=-1, keepdims=True)
    return xc * lax.rsqrt(var + LN_EPS) * g + b


def _pick_tile(n, pref):
    t = min(n, pref)
    while n % t:
        t //= 2
    return t


def _proj_kernel(x_ref, w_ref, o_ref):
    o_ref[...] = jnp.dot(x_ref[...].astype(BF16), w_ref[...], preferred_element_type=F32)


def _proj(x2d, w_bf16, tm, tn):
    n, d = x2d.shape
    nw = w_bf16.shape[1]
    return pl.pallas_call(
        _proj_kernel,
        grid=(n // tm, nw // tn),
        in_specs=[pl.BlockSpec((tm, d), lambda i, j: (i, 0)),
                  pl.BlockSpec((d, tn), lambda i, j: (0, j))],
        out_specs=pl.BlockSpec((tm, tn), lambda i, j: (i, j)),
        out_shape=jax.ShapeDtypeStruct((n, nw), F32),
        compiler_params=_cparams(("parallel", "arbitrary")),
        name="in_proj",
    )(x2d, w_bf16)


def _rows(start, size, stride):
    return pl.ds(start, size) if stride == 1 else pl.ds(start, size, stride=stride)


def _attn_prompt_kernel(slopes_ref, q_ref, k_ref, v_ref, o_ref, m_ref, l_ref, *, seq, qb, scale):
    hs = pl.program_id(1)
    g = pl.program_id(2)
    n_groups = len(DILATIONS)
    for gi, d in enumerate(DILATIONS):
        @pl.when(g == gi)
        def _(gi=gi, d=d):
            slope = slopes_ref[gi * HEADS_PER_GROUP + hs] * float(d)
            for r in range(d):
                for i in range(seq // (d * qb)):
                    rq = _rows(r + i * qb * d, qb, d)
                    q = q_ref[0, rq, :].astype(BF16)
                    if i == 0:
                        rk, nk, koff = rq, qb, 0
                    else:
                        rk, nk, koff = _rows(r + (i - 1) * qb * d, 2 * qb, d), 2 * qb, qb
                    kk = k_ref[0, rk, :].astype(BF16)
                    vv = v_ref[0, rk, :].astype(BF16)
                    s = lax.dot_general(q, kk, NT_DIMS, preferred_element_type=F32) * scale
                    row = lax.broadcasted_iota(I32, (qb, nk), 0)
                    col = lax.broadcasted_iota(I32, (qb, nk), 1)
                    j = row + koff - col
                    s = s - slope * j.astype(F32)
                    s = jnp.where((j >= 0) & (j <= qb), s, NEG)
                    m_blk = jnp.max(s, axis=1, keepdims=True)
                    p = jnp.exp(s - m_blk)
                    l_blk = jnp.sum(p, axis=1, keepdims=True)
                    o_blk = jnp.dot(p.astype(BF16), vv, preferred_element_type=F32)
                    m_b = jnp.broadcast_to(m_blk, (qb, HEAD_DIM))
                    l_b = jnp.broadcast_to(l_blk, (qb, HEAD_DIM))
                    if gi == 0:
                        m_new, l_new, acc = m_b, l_b, o_blk
                    else:
                        m_old = m_ref[rq, :]
                        m_new = jnp.maximum(m_old, m_b)
                        a_old = jnp.exp(m_old - m_new)
                        a_blk = jnp.exp(m_b - m_new)
                        l_new = a_old * l_ref[rq, :] + a_blk * l_b
                        acc = a_old * o_ref[0, rq, :] + a_blk * o_blk
                    if gi == n_groups - 1:
                        o_ref[0, rq, :] = acc / l_new
                    else:
                        m_ref[rq, :] = m_new
                        l_ref[rq, :] = l_new
                        o_ref[0, rq, :] = acc


def _attn_prompt(z3, slopes, ssm_width, qkv_width, qb):
    bsz, seq, _ = z3.shape
    blk = lambda base: (lambda b, hs, g: (b, 0, base + g * HEADS_PER_GROUP + hs))
    qc, kc, vc = (ssm_width // HEAD_DIM, (ssm_width + qkv_width) // HEAD_DIM,
                  (ssm_width + 2 * qkv_width) // HEAD_DIM)
    return pl.pallas_call(
        functools.partial(_attn_prompt_kernel, seq=seq, qb=qb, scale=HEAD_DIM ** -0.5),
        grid=(bsz, HEADS_PER_GROUP, len(DILATIONS)),
        in_specs=[pl.BlockSpec(memory_space=pltpu.SMEM),
                  pl.BlockSpec((1, seq, HEAD_DIM), blk(qc)),
                  pl.BlockSpec((1, seq, HEAD_DIM), blk(kc)),
                  pl.BlockSpec((1, seq, HEAD_DIM), blk(vc))],
        out_specs=pl.BlockSpec((1, seq, HEAD_DIM), lambda b, hs, g: (b, 0, hs)),
        out_shape=jax.ShapeDtypeStruct((bsz, seq, HEADS_PER_GROUP * HEAD_DIM), F32),
        scratch_shapes=[pltpu.VMEM((seq, HEAD_DIM), F32), pltpu.VMEM((seq, HEAD_DIM), F32)],
        compiler_params=_cparams(("parallel", "parallel", "arbitrary")),
        name="attn_prompt",
    )(slopes, z3, z3, z3)


def _attn_sample_kernel(slopes_ref, q_ref, k_ref, v_ref, c0_ref, c1_ref, c2_ref, o_ref, *, n_new, scale):
    caches = (c0_ref, c1_ref, c2_ref)
    kv_half = HEADS_PER_GROUP * HEAD_DIM
    for hs in range(HEADS_PER_GROUP):
        pieces = []
        for gi, d in enumerate(DILATIONS):
            c_ref = caches[gi]
            win = c_ref.shape[1]
            hcol = (gi * HEADS_PER_GROUP + hs) * HEAD_DIM
            slope = slopes_ref[gi * HEADS_PER_GROUP + hs]
            q = q_ref[0, :, hcol:hcol + HEAD_DIM].astype(BF16)
            k_new = k_ref[0, :, hcol:hcol + HEAD_DIM].astype(BF16)
            v_new = v_ref[0, :, hcol:hcol + HEAD_DIM].astype(BF16)
            k_old = c_ref[0, :, hs * HEAD_DIM:(hs + 1) * HEAD_DIM].astype(BF16)
            v_old = c_ref[0, :, kv_half + hs * HEAD_DIM:kv_half + (hs + 1) * HEAD_DIM].astype(BF16)
            for kk, vv, nk, base in ((k_old, v_old, win, win), (k_new, v_new, n_new, 0)):
                s = lax.dot_general(q, kk, NT_DIMS, preferred_element_type=F32) * scale
                row = lax.broadcasted_iota(I32, (n_new, nk), 0)
                col = lax.broadcasted_iota(I32, (n_new, nk), 1)
                offs = base + row - col
                valid = (offs >= 0) & (offs <= win) & ((offs & (d - 1)) == 0)
                s = jnp.where(valid, s - slope * offs.astype(F32), NEG)
                pieces.append((s, vv))
        m = pieces[0][0].max(axis=1, keepdims=True)
        for s, _ in pieces[1:]:
            m = jnp.maximum(m, s.max(axis=1, keepdims=True))
        l = jnp.zeros((n_new, 1), F32)
        acc = jnp.zeros((n_new, HEAD_DIM), F32)
        for s, vv in pieces:
            p = jnp.exp(s - m)
            l = l + p.sum(axis=1, keepdims=True)
            acc = acc + jnp.dot(p.astype(BF16), vv, preferred_element_type=F32)
        o_ref[0, :, hs * HEAD_DIM:(hs + 1) * HEAD_DIM] = acc / l


def _attn_sample(slopes, q, k, v, caches):
    bsz, n_new, qkv_width = q.shape
    assert all(d & (d - 1) == 0 for d in DILATIONS)
    row_spec = lambda a: pl.BlockSpec((1,) + a.shape[1:], lambda b: (b, 0, 0))
    return pl.pallas_call(
        functools.partial(_attn_sample_kernel, n_new=n_new, scale=HEAD_DIM ** -0.5),
        grid=(bsz,),
        in_specs=[pl.BlockSpec(memory_space=pltpu.SMEM), row_spec(q), row_spec(k), row_spec(v)]
                 + [row_spec(c) for c in caches],
        out_specs=pl.BlockSpec((1, n_new, HEADS_PER_GROUP * HEAD_DIM), lambda b: (b, 0, 0)),
        out_shape=jax.ShapeDtypeStruct((bsz, n_new, HEADS_PER_GROUP * HEAD_DIM), F32),
        compiler_params=_cparams(("parallel",)),
        name="attn_sample",
    )(slopes, q, k, v, *caches)


def _ssm_prep_kernel(ldt_ref, ar_r_ref, ai_r_ref, ar_c_ref, ai_c_ref, br_ref, bi_ref, cr_ref, ci_ref,
                     kt_ref, e_ref, f_ref, at_ref, *, chunk):
    dt = jnp.exp(ldt_ref[0])

    def discretise(ar, ai):
        mag = jnp.exp(ar * dt)
        return mag * jnp.cos(ai * dt), mag * jnp.sin(ai * dt)

    ar_r, ai_r = ar_r_ref[0], ai_r_ref[0]
    ar_c, ai_c = ar_c_ref[0], ai_c_ref[0]
    abr_r, abi_r = discretise(ar_r, ai_r)
    abr_c, abi_c = discretise(ar_c, ai_c)
    den = ar_c * ar_c + ai_c * ai_c
    f_re = ((abr_c - 1.0) * ar_c + abi_c * ai_c) / den
    f_im = (abi_c * ar_c - (abr_c - 1.0) * ai_c) / den
    b_re, b_im = br_ref[0], bi_ref[0]
    bb_re = f_re * b_re - f_im * b_im
    bb_im = f_re * b_im + f_im * b_re
    c_re, c_im = cr_ref[0], ci_ref[0]

    def powers(re1, im1):
        pr, pi = [jnp.ones_like(re1)], [jnp.zeros_like(re1)]
        for _ in range(chunk):
            pr.append(pr[-1] * re1 - pi[-1] * im1)
            pi.append(pr[-2] * im1 + pi[-1] * re1)
        return pr, pi

    pr_r, pi_r = powers(abr_r, abi_r)
    pr_c, pi_c = powers(abr_c, abi_c)
    hi = lax.Precision.HIGHEST
    for t in range(chunk):
        cp_re = c_re * pr_r[t] - c_im * pi_r[t]
        cp_im = c_re * pi_r[t] + c_im * pr_r[t]
        kt_ref[0, t] = (jnp.dot(cp_re, bb_re, precision=hi, preferred_element_type=F32)
                        - jnp.dot(cp_im, bb_im, precision=hi, preferred_element_type=F32))
        qr, qi = pr_c[chunk - 1 - t], pi_c[chunk - 1 - t]
        e_ref[0, t, 0] = qr * bb_re - qi * bb_im
        e_ref[0, t, 1] = qr * bb_im + qi * bb_re
        f_ref[0, t, 0] = c_re * pr_r[t + 1] - c_im * pi_r[t + 1]
        f_ref[0, t, 1] = -(c_re * pi_r[t + 1] + c_im * pr_r[t + 1])
    at_ref[0, 0:1, :] = pr_r[chunk]
    at_ref[0, 1:2, :] = pi_r[chunk]


def _ssm_prep(log_dt, a_re, a_im, b_re, b_im, c_re, c_im, chunk):
    g, p = a_re.shape
    h = b_re.shape[2]
    g3 = lambda *s: pl.BlockSpec((1,) + s, lambda i: (i,) + (0,) * len(s))
    return pl.pallas_call(
        functools.partial(_ssm_prep_kernel, chunk=chunk),
        grid=(g,),
        in_specs=[g3(1, 1), g3(1, p), g3(1, p), g3(p, 1), g3(p, 1), g3(p, h), g3(p, h), g3(h, p), g3(h, p)],
        out_specs=[g3(chunk, h, h), g3(chunk, 2, p, h), g3(chunk, 2, h, p), g3(2, p)],
        out_shape=[jax.ShapeDtypeStruct((g, chunk, h, h), F32),
                   jax.ShapeDtypeStruct((g, chunk, 2, p, h), F32),
                   jax.ShapeDtypeStruct((g, chunk, 2, h, p), F32),
                   jax.ShapeDtypeStruct((g, 2, p), F32)],
        compiler_params=_cparams(("parallel",)),
        name="ssm_prep",
    )(log_dt.reshape(g, 1, 1), a_re.reshape(g, 1, p), a_im.reshape(g, 1, p),
      a_re.reshape(g, p, 1), a_im.reshape(g, p, 1), b_re, b_im, c_re, c_im)


def _ssm_layout(kt, e, f, at, chunk):
    g, _, h, _ = kt.shape
    p = at.shape[2]
    ns = g // SSM_PACK
    eye = jnp.eye(SSM_PACK, dtype=F32)
    lag = jnp.arange(chunk)[None, :] - jnp.arange(chunk)[:, None]
    ktoe = jnp.where((lag >= 0)[None, :, :, None, None],
                     kt[:, jnp.clip(lag, 0, chunk - 1)], 0.0)
    ktoe = ktoe.reshape(ns, SSM_PACK, chunk, chunk, h, h)
    m = jnp.einsum('ogijkh,gq->oighjqk', ktoe, eye).reshape(ns, chunk * SSM_PACK * h, chunk * SSM_PACK * h)
    e6 = e.reshape(ns, SSM_PACK, chunk, 2, p, h)
    em = jnp.einsum('ogirph,gq->oighrqp', e6, eye).reshape(ns, chunk * SSM_PACK * h, 2 * SSM_PACK * p)
    f6 = f.reshape(ns, SSM_PACK, chunk, 2, h, p)
    fm = jnp.einsum('ogirhp,gq->orgpiqh', f6, eye).reshape(ns, 2 * SSM_PACK * p, chunk * SSM_PACK * h)
    am = at.reshape(ns, SSM_PACK, 2, p).transpose(0, 2, 1, 3).reshape(ns, 2, SSM_PACK * p)
    return m.astype(BF16), em, fm.astype(BF16), am


def _ssm_kernel(u_ref, m_ref, e_ref, f_ref, a_ref, h0_ref, y_ref, hfin_ref, h_scr, s_scr, hp_scr,
                *, bt, n_chunks, exact_state):
    half = a_ref.shape[2]

    @pl.when(pl.program_id(1) == 0)
    def _():
        h_scr[...] = h0_ref[0]

    u = u_ref[...]
    if exact_state:
        s_scr[...] = jnp.dot(u, e_ref[0], precision=lax.Precision.HIGHEST, preferred_element_type=F32)
    else:
        s_scr[...] = jnp.dot(u.astype(BF16), e_ref[0].astype(BF16), preferred_element_type=F32)
    a_re = jnp.broadcast_to(a_ref[0, 0:1, :], (bt, half))
    a_im = jnp.broadcast_to(a_ref[0, 1:2, :], (bt, half))

    def step(c, carry):
        h_re, h_im = carry
        rows = pl.ds(pl.multiple_of(c * bt, bt), bt)
        hp_scr[rows, 0:half] = h_re
        hp_scr[rows, half:2 * half] = h_im
        n_re = a_re * h_re - a_im * h_im + s_scr[rows, 0:half]
        n_im = a_re * h_im + a_im * h_re + s_scr[rows, half:2 * half]
        return n_re, n_im

    h_re, h_im = lax.fori_loop(0, n_chunks, step, (h_scr[:, 0:half], h_scr[:, half:2 * half]))
    h_scr[:, 0:half] = h_re
    h_scr[:, half:2 * half] = h_im
    hfin_ref[0] = h_scr[...]
    y_ref[...] = (jnp.dot(u.astype(BF16), m_ref[0], preferred_element_type=F32)
                  + jnp.dot(hp_scr[...].astype(BF16), f_ref[0], preferred_element_type=F32))


def _ssm(u2, m, em, fm, am, h0, bt, exact_state):
    rows, _ = u2.shape
    ns, sw, _ = m.shape
    st = em.shape[2]
    tr = _pick_tile(rows // bt, 64) * bt
    slab = lambda i, r: (i, 0, 0)
    return pl.pallas_call(
        functools.partial(_ssm_kernel, bt=bt, n_chunks=tr // bt, exact_state=exact_state),
        grid=(ns, rows // tr),
        in_specs=[pl.BlockSpec((tr, sw), lambda i, r: (r, i)),
                  pl.BlockSpec((1, sw, sw), slab), pl.BlockSpec((1, sw, st), slab),
                  pl.BlockSpec((1, st, sw), slab), pl.BlockSpec((1, 2, st // 2), slab),
                  pl.BlockSpec((1, bt, st), slab)],
        out_specs=[pl.BlockSpec((tr, sw), lambda i, r: (r, i)), pl.BlockSpec((1, bt, st), slab)],
        out_shape=[jax.ShapeDtypeStruct((rows, ns * sw), F32), jax.ShapeDtypeStruct((ns, bt, st), F32)],
        scratch_shapes=[pltpu.VMEM((bt, st), F32), pltpu.VMEM((tr, st), F32), pltpu.VMEM((tr, st), F32)],
        compiler_params=_cparams(("parallel", "arbitrary")),
        name="ssm_scan",
    )(u2, m, em, fm, am, h0)


def _to_chunk_rows(u, chunk, lanes):
    b, l, w = u.shape
    c = l // chunk
    return u.reshape(b, c, chunk, w // lanes, lanes).transpose(1, 0, 3, 2, 4).reshape(c * b, chunk * w)


def _from_chunk_rows(y2, b, chunk, lanes):
    rows, cw = y2.shape
    c, w = rows // b, cw // chunk
    return y2.reshape(c, b, w // lanes, chunk, lanes).transpose(1, 0, 3, 2, 4).reshape(b * c * chunk, w)


def _state_to_slabs(h, ns):
    b, g, p, _ = h.shape
    return h.reshape(b, ns, SSM_PACK, p, 2).transpose(1, 0, 4, 2, 3).reshape(ns, b, 2 * SSM_PACK * p)


def _state_from_slabs(hs, p):
    ns, b, _ = hs.shape
    return hs.reshape(ns, b, 2, SSM_PACK, p).transpose(1, 0, 3, 4, 2).reshape(b, ns * SSM_PACK, p, 2)


def _merge_kernel(yc_ref, u_ref, gs_ref, ga_ref, at_ref, x_ref, d_ref, wglu_ref, wbs_ref, wba_ref, wout_ref,
                  g1_ref, b1_ref, h_ref):
    y = yc_ref[...] + d_ref[...] * u_ref[...]
    gl = _gelu_tanh(y)
    so = gl * _sigmoid(jnp.dot(gl.astype(BF16), wglu_ref[...], preferred_element_type=F32))
    merged = (_sigmoid(gs_ref[...]) * jnp.dot(so.astype(BF16), wbs_ref[...], preferred_element_type=F32)
              + _sigmoid(ga_ref[...]) * jnp.dot(at_ref[...].astype(BF16), wba_ref[...], preferred_element_type=F32))
    mix = jnp.dot(merged.astype(BF16), wout_ref[...], preferred_element_type=F32)
    h_ref[...] = _layer_norm(DEEPNORM_ALPHA * x_ref[...] + mix, g1_ref[...], b1_ref[...])


def _merge(yc, z, gates, attn, x2d, d_skip, w_glu, w_bs, w_ba, w_out, ln_g, ln_b, tm):
    n, dm = x2d.shape
    aw = attn.shape[1]
    ssm_width = yc.shape[1]
    row = lambda w, cb=0: pl.BlockSpec((tm, w), lambda i: (i, cb))
    full = lambda a: pl.BlockSpec(a.shape, lambda i: (0, 0), pipeline_mode=pl.Buffered(1))
    return pl.pallas_call(
        _merge_kernel,
        grid=(n // tm,),
        in_specs=[row(ssm_width), row(ssm_width), row(dm, 0), row(dm, 1), row(aw), row(dm),
                  full(d_skip), full(w_glu), full(w_bs), full(w_ba), full(w_out), full(ln_g), full(ln_b)],
        out_specs=row(dm),
        out_shape=jax.ShapeDtypeStruct((n, dm), F32),
        compiler_params=_cparams(("parallel",)),
        name="merge",
    )(yc, z, gates, gates, attn, x2d, d_skip, w_glu, w_bs, w_ba, w_out, ln_g, ln_b)


def _order(x, y):
    if y is None:
        return x, None
    if x is None:
        return y, None
    (va, ta), (vb, tb) = x, y
    first = (va > vb) | ((va == vb) & (ta < tb))
    return ((jnp.where(first, va, vb), jnp.where(first, ta, tb)),
            (jnp.where(first, vb, va), jnp.where(first, tb, ta)))


def _bitonic_merge(xs):
    n = len(xs)
    if n == 1:
        return xs
    pairs = [_order(xs[i], xs[i + n // 2]) for i in range(n // 2)]
    return _bitonic_merge([p[0] for p in pairs]) + _bitonic_merge([p[1] for p in pairs])


def _sort_desc(xs):
    n = len(xs)
    if n == 1:
        return xs
    return _bitonic_merge(_sort_desc(xs[:n // 2]) + _sort_desc(xs[n // 2:])[::-1])


def _top_of_two(xs, ys):
    n = len(xs)
    return _bitonic_merge([_order(xs[i], ys[n - 1 - i])[0] for i in range(n)])


def _top_sorted(items, k):
    items = items + [None] * (-len(items) % k)
    group = lambda s: _sort_desc([it() if callable(it) else it for it in items[s:s + k]])
    run = group(0)
    for s in range(k, len(items), k):
        run = _top_of_two(run, group(s))
    return run


def _peer_topk_kernel(h_ref, wq_ref, k1_ref, k2_ref, a_ref, b_ref, g_ref, s_scr, o_scr, *, half, n_keys, tm):
    k = PEER_TOPK
    nj = tm // 128
    key_bits = n_keys.bit_length() - 1
    hi = lax.Precision.HIGHEST
    x = h_ref[...].astype(BF16)

    def half_top(keys_ref, qh):
        s = lax.dot_general(keys_ref[...], qh, NT_DIMS, precision=hi, preferred_element_type=F32)
        for kh in range(n_keys // 8):
            for j in range(nj):
                s_scr[pl.ds((kh * nj + j) * 8, 8), :] = s[kh * 8:(kh + 1) * 8, j * 128:(j + 1) * 128]
        def item(key):
            start = (key // 8) * nj * 8 + key % 8
            rows = pl.ds(start, nj, stride=8) if nj > 1 else pl.ds(start, 1)
            return lambda: (s_scr[rows, :], jnp.full((nj, 128), key, I32))

        return _top_sorted([item(key) for key in range(n_keys)], k)

    for hd in range(PEER_HEADS):
        q = jnp.dot(x, wq_ref[:, hd * 2 * half:(hd + 1) * 2 * half], preferred_element_type=F32)
        t1 = half_top(k1_ref, q[:, :half])
        t2 = half_top(k2_ref, q[:, half:])
        cands = []
        for r in range(k):
            for c in range(k):
                if (r + 1) * (c + 1) <= k:
                    tag = (t1[r][1] << key_bits) + t2[c][1] + ((r * k + c) << (2 * key_bits))
                    cands.append((t1[r][0] + t2[c][0], tag))
        top = _top_sorted(cands, k)
        ex = [jnp.exp(v - top[0][0]) for v, _ in top]
        tot = ex[0]
        for e in ex[1:]:
            tot = tot + e
        for i, (_, tag) in enumerate(top):
            row = pl.ds((hd * k + i) * nj, nj)
            o_scr[0, row, :] = ((tag >> key_bits) & (n_keys - 1)).astype(F32)
            o_scr[1, row, :] = (tag & (n_keys - 1)).astype(F32)
            o_scr[2, row, :] = ex[i] / tot
    ne = PEER_HEADS * k
    for j in range(nj):
        rows = pl.ds(j, ne, stride=nj) if nj > 1 else pl.ds(0, ne)
        tok = pl.ds(j * 128, 128)
        a_ref[tok, :] = o_scr[0, rows, :].T.astype(I32)
        b_ref[tok, :] = o_scr[1, rows, :].T.astype(I32)
        g_ref[tok, :] = o_scr[2, rows, :].T


def _peer_topk(h, wq_bf16, k1, k2, tm):
    n, dm = h.shape
    n_keys, half = k1.shape
    ne = PEER_HEADS * PEER_TOPK
    assert n_keys & (n_keys - 1) == 0 and tm % 128 == 0 and ne == 128
    assert (PEER_TOPK * PEER_TOPK) << (2 * (n_keys.bit_length() - 1)) < 2 ** 31
    full = lambda a: pl.BlockSpec(a.shape, lambda i: (0, 0), pipeline_mode=pl.Buffered(1))
    out = lambda: pl.BlockSpec((tm, ne), lambda i: (i, 0))
    return pl.pallas_call(
        functools.partial(_peer_topk_kernel, half=half, n_keys=n_keys, tm=tm),
        grid=(n // tm,),
        in_specs=[pl.BlockSpec((tm, dm), lambda i: (i, 0)), full(wq_bf16), full(k1), full(k2)],
        out_specs=[out(), out(), out()],
        out_shape=[jax.ShapeDtypeStruct((n, ne), I32), jax.ShapeDtypeStruct((n, ne), I32),
                   jax.ShapeDtypeStruct((n, ne), F32)],
        scratch_shapes=[pltpu.VMEM((n_keys * tm // 128, 128), F32), pltpu.VMEM((3, ne * tm // 128, 128), F32)],
        compiler_params=_cparams(("parallel",)),
        name="peer_topk",
    )(h, wq_bf16, k1, k2)


def _peer_dense_kernel(a_ref, b_ref, g_ref, h_ref, u_ref, v_ref, g2_ref, b2_ref, y_ref, w_scr, x_scr, acc_scr, p_scr,
                       *, tm, n_keys, rows_per_step):
    e_blk = pl.program_id(1)
    n_e = pl.num_programs(1) - 1
    half = tm // 2
    ne = a_ref.shape[1]
    u32 = jnp.uint32

    @pl.when(e_blk == 0)
    def _():
        x_scr[...] = h_ref[...].astype(BF16)
        acc_scr[...] = jnp.zeros_like(acc_scr)
        p_scr[1] = jnp.zeros(p_scr.shape[1:], BF16)
        key_iota = lax.broadcasted_iota(I32, (n_keys, ne), 0)

        def gate_bits(t):
            a_row = a_ref[pl.ds(t, 1), :]
            b_row = b_ref[pl.ds(t, 1), :]
            g_row = g_ref[pl.ds(t, 1), :]
            sel_a = jnp.where(key_iota == a_row, 1.0, 0.0).astype(BF16)
            sel_b = jnp.where(key_iota == b_row, g_row, 0.0).astype(BF16)
            w = lax.dot_general(sel_a, sel_b, NT_DIMS, preferred_element_type=F32)
            return lax.bitcast_convert_type(w.astype(BF16).astype(F32), u32)

        def build(t, carry):
            word = gate_bits(t) | (gate_bits(t + half) >> 16)
            w_scr[pl.ds(pl.multiple_of(t * n_keys, n_keys), n_keys), :] = word
            return carry

        lax.fori_loop(0, half, build, 0, unroll=2)

    slot = e_blk % 2
    acc_scr[...] += jnp.dot(p_scr[1 - slot], v_ref[...], preferred_element_type=F32)

    z = lax.dot_general(x_scr[...], u_ref[...], NT_DIMS, preferred_element_type=F32)
    first_row = jnp.minimum(e_blk, n_e - 1) * rows_per_step
    for c in range(rows_per_step):
        word = w_scr[pl.ds(first_row + c, half, stride=n_keys), :]
        w_a = lax.bitcast_convert_type(word & u32(0xFFFF0000), F32)
        w_b = lax.bitcast_convert_type(word << 16, F32)
        cols = slice(c * n_keys, (c + 1) * n_keys)
        p_scr[slot, 0:half, cols] = (w_a * _gelu_tanh(z[0:half, cols])).astype(BF16)
        p_scr[slot, half:tm, cols] = (w_b * _gelu_tanh(z[half:tm, cols])).astype(BF16)

    @pl.when(e_blk == n_e)
    def _():
        y_ref[...] = _layer_norm(DEEPNORM_ALPHA * h_ref[...] + acc_scr[...], g2_ref[...], b2_ref[...])


def _peer_dense(a, b, g, h, u_bf16, v_bf16, ln_g, ln_b, n_keys, tm, rows_per_step):
    n, dm = h.shape
    ne = a.shape[1]
    eb = rows_per_step * n_keys
    n_e = n_keys // rows_per_step
    tok = lambda w: pl.BlockSpec((tm, w), lambda i, e: (i, 0))
    full = lambda arr: pl.BlockSpec(arr.shape, lambda i, e: (0, 0))
    return pl.pallas_call(
        functools.partial(_peer_dense_kernel, tm=tm, n_keys=n_keys, rows_per_step=rows_per_step),
        grid=(n // tm, n_e + 1),
        in_specs=[tok(ne), tok(ne), tok(ne), tok(dm),
                  pl.BlockSpec((eb, dm), lambda i, e: (jnp.minimum(e, n_e - 1), 0)),
                  pl.BlockSpec((eb, dm), lambda i, e: (jnp.maximum(e - 1, 0), 0)),
                  full(ln_g), full(ln_b)],
        out_specs=tok(dm),
        out_shape=jax.ShapeDtypeStruct((n, dm), F32),
        scratch_shapes=[pltpu.VMEM((tm // 2 * n_keys, n_keys), jnp.uint32), pltpu.VMEM((tm, dm), BF16),
                        pltpu.VMEM((tm, dm), F32), pltpu.VMEM((2, tm, eb), BF16)],
        compiler_params=_cparams(("parallel", "arbitrary")),
        name="peer_experts",
    )(a, b, g, h, u_bf16, v_bf16, ln_g, ln_b)


def _layer(x, caches, h0, wts, ssm_mats):
    (w_in, ssm_d, w_glu, w_bs, w_ba, w_out, ln1_g, ln1_b, wq, k1, k2, pu, pv, ln2_g, ln2_b, slopes) = wts
    m, em, fm, am = ssm_mats
    bsz, seq, dm = x.shape
    n = bsz * seq
    ssm_width = ssm_d.shape[1]
    in_width = w_in.shape[1]
    qkv_width = (in_width - ssm_width - 2 * dm) // 3
    kv_half = HEADS_PER_GROUP * HEAD_DIM
    k_off, v_off = ssm_width + qkv_width, ssm_width + 2 * qkv_width
    x2d = x.reshape(n, dm)

    gate_off = ssm_width + 3 * qkv_width
    z = _proj(x2d, w_in[:, :gate_off], _pick_tile(n, 1024), _pick_tile(gate_off, 512))
    gates = _proj(x2d, w_in[:, gate_off:], _pick_tile(n, 1024), _pick_tile(2 * dm, 512))
    z3 = z.reshape(bsz, seq, gate_off)

    new_kv = []
    if caches is None:
        qb = seq // max(DILATIONS)
        attn = _attn_prompt(z3, slopes, ssm_width, qkv_width, qb).reshape(n, kv_half)
        for gi, d in enumerate(DILATIONS):
            keep = min(qb * d, seq)
            kg = z3[:, seq - keep:, k_off + gi * kv_half:k_off + (gi + 1) * kv_half]
            vg = z3[:, seq - keep:, v_off + gi * kv_half:v_off + (gi + 1) * kv_half]
            new_kv.append(jnp.stack([kg, vg], axis=2).reshape(bsz, keep, 2, HEADS_PER_GROUP, HEAD_DIM))
    else:
        q = z3[:, :, ssm_width:k_off]
        k = z3[:, :, k_off:v_off]
        v = z3[:, :, v_off:v_off + qkv_width]
        flat = [c.reshape(c.shape[0], c.shape[1], 2 * kv_half) for c in caches]
        attn = _attn_sample(slopes, q, k, v, flat).reshape(n, kv_half)
        for gi in range(len(DILATIONS)):
            kg = k[:, :, gi * kv_half:(gi + 1) * kv_half]
            vg = v[:, :, gi * kv_half:(gi + 1) * kv_half]
            new_kv.append(jnp.stack([kg, vg], axis=2).reshape(bsz, seq, 2, HEADS_PER_GROUP, HEAD_DIM))

    ns = m.shape[0]
    p_state = em.shape[2] // (2 * SSM_PACK)
    lanes = ssm_width // ns
    u3 = z3[:, :, :ssm_width]
    exact_state = h0 is not None
    u2 = _to_chunk_rows(u3 if exact_state else u3.astype(BF16), SSM_CHUNK, lanes)
    h0s = (_state_to_slabs(h0, ns) if exact_state
           else jnp.zeros((ns, bsz, 2 * SSM_PACK * p_state), F32))
    y2, hfin = _ssm(u2, m, em, fm, am, h0s, bsz, exact_state)
    yc = _from_chunk_rows(y2, bsz, SSM_CHUNK, lanes)
    ssm_state = _state_from_slabs(hfin, p_state)

    h = _merge(yc, z, gates, attn, x2d, ssm_d, w_glu, w_bs, w_ba, w_out, ln1_g, ln1_b, _pick_tile(n, 256))
    a, b, g = _peer_topk(h, wq, k1, k2, _pick_tile(n, 1024))
    y = _peer_dense(a, b, g, h, pu, pv, ln2_g, ln2_b, k1.shape[0], _pick_tile(n, 512), 4)
    return y.reshape(bsz, seq, dm), new_kv, ssm_state


def kernel(x_prompt, x_sample, cache_kv_w128, cache_kv_w512, cache_kv_w2048, state_ssm, w_in, ssm_log_dt, ssm_a_re, ssm_a_im, ssm_b_re, ssm_b_im, ssm_c_re, ssm_c_im, ssm_d, ssm_w_glu, w_branch_ssm, w_branch_attn, w_out, ln1_g, ln1_b, peer_w_q, peer_sub_keys_1, peer_sub_keys_2, peer_u, peer_v, ln2_g, ln2_b):
    caches = (cache_kv_w128, cache_kv_w512, cache_kv_w2048)
    seq = x_prompt.shape[1]
    assert x_sample.shape[1] == SSM_CHUNK and seq % (SSM_CHUNK * max(DILATIONS)) == 0
    assert all(c.shape[1] == (seq // max(DILATIONS)) * d and c.shape[1] <= PAST_LEN
               for c, d in zip(caches, DILATIONS))
    n_heads = len(DILATIONS) * HEADS_PER_GROUP
    slopes = 2.0 ** (-8.0 * jnp.arange(1, n_heads + 1, dtype=F32) / n_heads)
    row = lambda a: a.reshape(1, -1).astype(F32)
    wts = (w_in.astype(BF16), row(ssm_d), ssm_w_glu.astype(BF16), w_branch_ssm.astype(BF16),
           w_branch_attn.astype(BF16), w_out.astype(BF16), row(ln1_g), row(ln1_b), peer_w_q.astype(BF16),
           peer_sub_keys_1.astype(F32), peer_sub_keys_2.astype(F32), peer_u.astype(BF16), peer_v.astype(BF16),
           row(ln2_g), row(ln2_b), slopes)
    ssm_mats = _ssm_layout(*_ssm_prep(ssm_log_dt, ssm_a_re, ssm_a_im, ssm_b_re, ssm_b_im, ssm_c_re, ssm_c_im,
                                      SSM_CHUNK), SSM_CHUNK)
    y_p, kv_p, ssm_p = _layer(x_prompt, None, None, wts, ssm_mats)
    y_s, kv_s, ssm_s = _layer(x_sample, caches, state_ssm, wts, ssm_mats)
    return (y_p, y_s, kv_p[0], kv_p[1], kv_p[2], ssm_p, kv_s[0], kv_s[1], kv_s[2], ssm_s)
```

```python
import functools
import math

import jax
import jax.numpy as jnp
from jax import lax
from jax.experimental import pallas as pl
from jax.experimental.pallas import tpu as pltpu

F32 = jnp.float32
BF16 = jnp.bfloat16
I32 = jnp.int32

HEAD_DIM = 128
HEADS_PER_GROUP = 4
DILATIONS = (1, 4, 16)
PAST_LEN = 16384
PEER_HEADS = 8
PEER_TOPK = 16
DEPTH = 1
DEEPNORM_ALPHA = (2.0 * DEPTH) ** 0.25
LN_EPS = 1e-5
SSM_CHUNK = 8
SSM_PACK = 8
NEG = -1e30
VMEM_LIMIT = 56 * 1024 * 1024

NT_DIMS = (((1,), (1,)), ((), ()))


def _cparams(sem):
    return pltpu.CompilerParams(dimension_semantics=sem, vmem_limit_bytes=VMEM_LIMIT)


def _gelu_tanh(x):
    return 0.5 * x * (1.0 + jnp.tanh(math.sqrt(2.0 / math.pi) * (x + 0.044715 * (x * x * x))))


def _sigmoid(x):
    return 1.0 / (1.0 + jnp.exp(-x))


def _layer_norm(x, g, b):
    mu = jnp.mean(x, axis=-1, keepdims=True)
    xc = x - mu
    var = jnp.mean(xc * xc, axis=-1, keepdims=True)
    return xc * lax.rsqrt(var + LN_EPS) * g + b


def _pick_tile(n, pref):
    t = min(n, pref)
    while n % t:
        t //= 2
    return t


def _proj_kernel(x_ref, w_ref, o_ref):
    o_ref[...] = jnp.dot(x_ref[...].astype(BF16), w_ref[...], preferred_element_type=F32)


def _proj(x2d, w_bf16, tm, tn):
    n, d = x2d.shape
    nw = w_bf16.shape[1]
    return pl.pallas_call(
        _proj_kernel,
        grid=(n // tm, nw // tn),
        in_specs=[pl.BlockSpec((tm, d), lambda i, j: (i, 0)),
                  pl.BlockSpec((d, tn), lambda i, j: (0, j))],
        out_specs=pl.BlockSpec((tm, tn), lambda i, j: (i, j)),
        out_shape=jax.ShapeDtypeStruct((n, nw), F32),
        compiler_params=_cparams(("parallel", "arbitrary")),
        name="in_proj",
    )(x2d, w_bf16)


def _rows(start, size, stride):
    return pl.ds(start, size) if stride == 1 else pl.ds(start, size, stride=stride)


def _attn_prompt_kernel(slopes_ref, q_ref, k_ref, v_ref, o_ref, m_ref, l_ref, *, seq, qb, scale):
    hs = pl.program_id(1)
    g = pl.program_id(2)
    n_groups = len(DILATIONS)
    for gi, d in enumerate(DILATIONS):
        @pl.when(g == gi)
        def _(gi=gi, d=d):
            slope = slopes_ref[gi * HEADS_PER_GROUP + hs] * float(d)
            for r in range(d):
                for i in range(seq // (d * qb)):
                    rq = _rows(r + i * qb * d, qb, d)
                    q = q_ref[0, rq, :].astype(BF16)
                    if i == 0:
                        rk, nk, koff = rq, qb, 0
                    else:
                        rk, nk, koff = _rows(r + (i - 1) * qb * d, 2 * qb, d), 2 * qb, qb
                    kk = k_ref[0, rk, :].astype(BF16)
                    vv = v_ref[0, rk, :].astype(BF16)
                    s = lax.dot_general(q, kk, NT_DIMS, preferred_element_type=F32) * scale
                    row = lax.broadcasted_iota(I32, (qb, nk), 0)
                    col = lax.broadcasted_iota(I32, (qb, nk), 1)
                    j = row + koff - col
                    s = s - slope * j.astype(F32)
                    s = jnp.where((j >= 0) & (j <= qb), s, NEG)
                    m_blk = jnp.max(s, axis=1, keepdims=True)
                    p = jnp.exp(s - m_blk)
                    l_blk = jnp.sum(p, axis=1, keepdims=True)
                    o_blk = jnp.dot(p.astype(BF16), vv, preferred_element_type=F32)
                    m_b = jnp.broadcast_to(m_blk, (qb, HEAD_DIM))
                    l_b = jnp.broadcast_to(l_blk, (qb, HEAD_DIM))
                    if gi == 0:
                        m_new, l_new, acc = m_b, l_b, o_blk
                    else:
                        m_old = m_ref[rq, :]
                        m_new = jnp.maximum(m_old, m_b)
                        a_old = jnp.exp(m_old - m_new)
                        a_blk = jnp.exp(m_b - m_new)
                        l_new = a_old * l_ref[rq, :] + a_blk * l_b
                        acc = a_old * o_ref[0, rq, :] + a_blk * o_blk
                    if gi == n_groups - 1:
                        o_ref[0, rq, :] = acc / l_new
                    else:
                        m_ref[rq, :] = m_new
                        l_ref[rq, :] = l_new
                        o_ref[0, rq, :] = acc


def _attn_prompt(z3, slopes, ssm_width, qkv_width, qb):
    bsz, seq, _ = z3.shape
    blk = lambda base: (lambda b, hs, g: (b, 0, base + g * HEADS_PER_GROUP + hs))
    qc, kc, vc = (ssm_width // HEAD_DIM, (ssm_width + qkv_width) // HEAD_DIM,
                  (ssm_width + 2 * qkv_width) // HEAD_DIM)
    return pl.pallas_call(
        functools.partial(_attn_prompt_kernel, seq=seq, qb=qb, scale=HEAD_DIM ** -0.5),
        grid=(bsz, HEADS_PER_GROUP, len(DILATIONS)),
        in_specs=[pl.BlockSpec(memory_space=pltpu.SMEM),
                  pl.BlockSpec((1, seq, HEAD_DIM), blk(qc)),
                  pl.BlockSpec((1, seq, HEAD_DIM), blk(kc)),
                  pl.BlockSpec((1, seq, HEAD_DIM), blk(vc))],
        out_specs=pl.BlockSpec((1, seq, HEAD_DIM), lambda b, hs, g: (b, 0, hs)),
        out_shape=jax.ShapeDtypeStruct((bsz, seq, HEADS_PER_GROUP * HEAD_DIM), F32),
        scratch_shapes=[pltpu.VMEM((seq, HEAD_DIM), F32), pltpu.VMEM((seq, HEAD_DIM), F32)],
        compiler_params=_cparams(("parallel", "parallel", "arbitrary")),
        name="attn_prompt",
    )(slopes, z3, z3, z3)


def _attn_sample_kernel(slopes_ref, q_ref, k_ref, v_ref, c0_ref, c1_ref, c2_ref, o_ref, *, n_new, scale):
    caches = (c0_ref, c1_ref, c2_ref)
    for hs in range(HEADS_PER_GROUP):
        pieces = []
        for gi, d in enumerate(DILATIONS):
            c_ref = caches[gi]
            win = c_ref.shape[1]
            hcol = (gi * HEADS_PER_GROUP + hs) * HEAD_DIM
            slope = slopes_ref[gi * HEADS_PER_GROUP + hs]
            q = q_ref[0, :, hcol:hcol + HEAD_DIM].astype(BF16)
            k_new = k_ref[0, :, hcol:hcol + HEAD_DIM].astype(BF16)
            v_new = v_ref[0, :, hcol:hcol + HEAD_DIM].astype(BF16)
            k_old = c_ref[0, :, 0, hs, :].astype(BF16)
            v_old = c_ref[0, :, 1, hs, :].astype(BF16)
            for kk, vv, nk, base in ((k_old, v_old, win, win), (k_new, v_new, n_new, 0)):
                s = lax.dot_general(q, kk, NT_DIMS, preferred_element_type=F32) * scale
                row = lax.broadcasted_iota(I32, (n_new, nk), 0)
                col = lax.broadcasted_iota(I32, (n_new, nk), 1)
                offs = base + row - col
                valid = (offs >= 0) & (offs <= win) & ((offs & (d - 1)) == 0)
                s = jnp.where(valid, s - slope * offs.astype(F32), NEG)
                pieces.append((s, vv))
        m = pieces[0][0].max(axis=1, keepdims=True)
        for s, _ in pieces[1:]:
            m = jnp.maximum(m, s.max(axis=1, keepdims=True))
        l = jnp.zeros((n_new, 1), F32)
        acc = jnp.zeros((n_new, HEAD_DIM), F32)
        for s, vv in pieces:
            p = jnp.exp(s - m)
            l = l + p.sum(axis=1, keepdims=True)
            acc = acc + jnp.dot(p.astype(BF16), vv, preferred_element_type=F32)
        o_ref[0, :, hs * HEAD_DIM:(hs + 1) * HEAD_DIM] = acc / l


def _attn_sample(slopes, q, k, v, caches):
    bsz, n_new, qkv_width = q.shape
    assert all(d & (d - 1) == 0 for d in DILATIONS)
    row_spec = lambda a: pl.BlockSpec((1,) + a.shape[1:], lambda b: (b,) + (0,) * (a.ndim - 1))
    return pl.pallas_call(
        functools.partial(_attn_sample_kernel, n_new=n_new, scale=HEAD_DIM ** -0.5),
        grid=(bsz,),
        in_specs=[pl.BlockSpec(memory_space=pltpu.SMEM), row_spec(q), row_spec(k), row_spec(v)]
                 + [row_spec(c) for c in caches],
        out_specs=pl.BlockSpec((1, n_new, HEADS_PER_GROUP * HEAD_DIM), lambda b: (b, 0, 0)),
        out_shape=jax.ShapeDtypeStruct((bsz, n_new, HEADS_PER_GROUP * HEAD_DIM), F32),
        compiler_params=_cparams(("parallel",)),
        name="attn_sample",
    )(slopes, q, k, v, *caches)


def _ssm_prep_kernel(ldt_ref, ar_r_ref, ai_r_ref, ar_c_ref, ai_c_ref, br_ref, bi_ref, cr_ref, ci_ref,
                     kt_ref, e_ref, f_ref, at_ref, *, chunk):
    dt = jnp.exp(ldt_ref[0])

    def discretise(ar, ai):
        mag = jnp.exp(ar * dt)
        return mag * jnp.cos(ai * dt), mag * jnp.sin(ai * dt)

    ar_r, ai_r = ar_r_ref[0], ai_r_ref[0]
    ar_c, ai_c = ar_c_ref[0], ai_c_ref[0]
    abr_r, abi_r = discretise(ar_r, ai_r)
    abr_c, abi_c = discretise(ar_c, ai_c)
    den = ar_c * ar_c + ai_c * ai_c
    f_re = ((abr_c - 1.0) * ar_c + abi_c * ai_c) / den
    f_im = (abi_c * ar_c - (abr_c - 1.0) * ai_c) / den
    b_re, b_im = br_ref[0], bi_ref[0]
    bb_re = f_re * b_re - f_im * b_im
    bb_im = f_re * b_im + f_im * b_re
    c_re, c_im = cr_ref[0], ci_ref[0]

    def powers(re1, im1):
        pr, pi = [jnp.ones_like(re1)], [jnp.zeros_like(re1)]
        for _ in range(chunk):
            pr.append(pr[-1] * re1 - pi[-1] * im1)
            pi.append(pr[-2] * im1 + pi[-1] * re1)
        return pr, pi

    pr_r, pi_r = powers(abr_r, abi_r)
    pr_c, pi_c = powers(abr_c, abi_c)
    hi = lax.Precision.HIGHEST
    for t in range(chunk):
        cp_re = c_re * pr_r[t] - c_im * pi_r[t]
        cp_im = c_re * pi_r[t] + c_im * pr_r[t]
        kt_ref[0, t] = (jnp.dot(cp_re, bb_re, precision=hi, preferred_element_type=F32)
                        - jnp.dot(cp_im, bb_im, precision=hi, preferred_element_type=F32))
        qr, qi = pr_c[chunk - 1 - t], pi_c[chunk - 1 - t]
        e_ref[0, t, 0] = qr * bb_re - qi * bb_im
        e_ref[0, t, 1] = qr * bb_im + qi * bb_re
        f_ref[0, t, 0] = c_re * pr_r[t + 1] - c_im * pi_r[t + 1]
        f_ref[0, t, 1] = -(c_re * pi_r[t + 1] + c_im * pr_r[t + 1])
    at_ref[0, 0:1, :] = pr_r[chunk]
    at_ref[0, 1:2, :] = pi_r[chunk]


def _ssm_prep(log_dt, a_re, a_im, b_re, b_im, c_re, c_im, chunk):
    g, p = a_re.shape
    h = b_re.shape[2]
    g3 = lambda *s: pl.BlockSpec((1,) + s, lambda i: (i,) + (0,) * len(s))
    return pl.pallas_call(
        functools.partial(_ssm_prep_kernel, chunk=chunk),
        grid=(g,),
        in_specs=[g3(1, 1), g3(1, p), g3(1, p), g3(p, 1), g3(p, 1), g3(p, h), g3(p, h), g3(h, p), g3(h, p)],
        out_specs=[g3(chunk, h, h), g3(chunk, 2, p, h), g3(chunk, 2, h, p), g3(2, p)],
        out_shape=[jax.ShapeDtypeStruct((g, chunk, h, h), F32),
                   jax.ShapeDtypeStruct((g, chunk, 2, p, h), F32),
                   jax.ShapeDtypeStruct((g, chunk, 2, h, p), F32),
                   jax.ShapeDtypeStruct((g, 2, p), F32)],
        compiler_params=_cparams(("parallel",)),
        name="ssm_prep",
    )(log_dt.reshape(g, 1, 1), a_re.reshape(g, 1, p), a_im.reshape(g, 1, p),
      a_re.reshape(g, p, 1), a_im.reshape(g, p, 1), b_re, b_im, c_re, c_im)


def _ssm_layout(kt, e, f, at, chunk):
    g, _, h, _ = kt.shape
    p = at.shape[2]
    ns = g // SSM_PACK
    eye = jnp.eye(SSM_PACK, dtype=F32)
    lag = jnp.arange(chunk)[None, :] - jnp.arange(chunk)[:, None]
    ktoe = jnp.where((lag >= 0)[None, :, :, None, None],
                     kt[:, jnp.clip(lag, 0, chunk - 1)], 0.0)
    ktoe = ktoe.reshape(ns, SSM_PACK, chunk, chunk, h, h)
    m = jnp.einsum('ogijkh,gq->oighjqk', ktoe, eye).reshape(ns, chunk * SSM_PACK * h, chunk * SSM_PACK * h)
    e6 = e.reshape(ns, SSM_PACK, chunk, 2, p, h)
    em = jnp.einsum('ogirph,gq->oighrqp', e6, eye).reshape(ns, chunk * SSM_PACK * h, 2 * SSM_PACK * p)
    f6 = f.reshape(ns, SSM_PACK, chunk, 2, h, p)
    fm = jnp.einsum('ogirhp,gq->orgpiqh', f6, eye).reshape(ns, 2 * SSM_PACK * p, chunk * SSM_PACK * h)
    am = at.reshape(ns, SSM_PACK, 2, p).transpose(0, 2, 1, 3).reshape(ns, 2, SSM_PACK * p)
    return m.astype(BF16), em, fm.astype(BF16), am


def _ssm_kernel(u_ref, m_ref, e_ref, f_ref, a_ref, h0_ref, y_ref, hfin_ref, h_scr, s_scr, hp_scr,
                *, bt, n_chunks, exact_state):
    half = a_ref.shape[2]

    @pl.when(pl.program_id(1) == 0)
    def _():
        h_scr[...] = h0_ref[0]

    u = u_ref[...]
    if exact_state:
        s_scr[...] = jnp.dot(u, e_ref[0], precision=lax.Precision.HIGHEST, preferred_element_type=F32)
    else:
        s_scr[...] = jnp.dot(u.astype(BF16), e_ref[0].astype(BF16), preferred_element_type=F32)
    a_re = jnp.broadcast_to(a_ref[0, 0:1, :], (bt, half))
    a_im = jnp.broadcast_to(a_ref[0, 1:2, :], (bt, half))

    def step(c, carry):
        h_re, h_im = carry
        rows = pl.ds(pl.multiple_of(c * bt, bt), bt)
        hp_scr[rows, 0:half] = h_re
        hp_scr[rows, half:2 * half] = h_im
        n_re = a_re * h_re - a_im * h_im + s_scr[rows, 0:half]
        n_im = a_re * h_im + a_im * h_re + s_scr[rows, half:2 * half]
        return n_re, n_im

    h_re, h_im = lax.fori_loop(0, n_chunks, step, (h_scr[:, 0:half], h_scr[:, half:2 * half]))
    h_scr[:, 0:half] = h_re
    h_scr[:, half:2 * half] = h_im
    hfin_ref[0] = h_scr[...]
    y_ref[...] = (jnp.dot(u.astype(BF16), m_ref[0], preferred_element_type=F32)
                  + jnp.dot(hp_scr[...].astype(BF16), f_ref[0], preferred_element_type=F32))


def _ssm(u2, m, em, fm, am, h0, bt, exact_state):
    rows, _ = u2.shape
    ns, sw, _ = m.shape
    st = em.shape[2]
    tr = _pick_tile(rows // bt, 64) * bt
    slab = lambda i, r: (i, 0, 0)
    return pl.pallas_call(
        functools.partial(_ssm_kernel, bt=bt, n_chunks=tr // bt, exact_state=exact_state),
        grid=(ns, rows // tr),
        in_specs=[pl.BlockSpec((tr, sw), lambda i, r: (r, i)),
                  pl.BlockSpec((1, sw, sw), slab), pl.BlockSpec((1, sw, st), slab),
                  pl.BlockSpec((1, st, sw), slab), pl.BlockSpec((1, 2, st // 2), slab),
                  pl.BlockSpec((1, bt, st), slab)],
        out_specs=[pl.BlockSpec((tr, sw), lambda i, r: (r, i)), pl.BlockSpec((1, bt, st), slab)],
        out_shape=[jax.ShapeDtypeStruct((rows, ns * sw), F32), jax.ShapeDtypeStruct((ns, bt, st), F32)],
        scratch_shapes=[pltpu.VMEM((bt, st), F32), pltpu.VMEM((tr, st), F32), pltpu.VMEM((tr, st), F32)],
        compiler_params=_cparams(("parallel", "arbitrary")),
        name="ssm_scan",
    )(u2, m, em, fm, am, h0)


def _to_chunk_rows(u, chunk, lanes):
    b, l, w = u.shape
    c = l // chunk
    return u.reshape(b, c, chunk, w // lanes, lanes).transpose(1, 0, 3, 2, 4).reshape(c * b, chunk * w)


def _from_chunk_rows(y2, b, chunk, lanes):
    rows, cw = y2.shape
    c, w = rows // b, cw // chunk
    return y2.reshape(c, b, w // lanes, chunk, lanes).transpose(1, 0, 3, 2, 4).reshape(b * c * chunk, w)


def _state_to_slabs(h, ns):
    b, g, p, _ = h.shape
    return h.reshape(b, ns, SSM_PACK, p, 2).transpose(1, 0, 4, 2, 3).reshape(ns, b, 2 * SSM_PACK * p)


def _state_from_slabs(hs, p):
    ns, b, _ = hs.shape
    return hs.reshape(ns, b, 2, SSM_PACK, p).transpose(1, 0, 3, 4, 2).reshape(b, ns * SSM_PACK, p, 2)


def _merge_kernel(yc_ref, u_ref, gs_ref, ga_ref, at_ref, x_ref, d_ref, wglu_ref, wbs_ref, wba_ref, wout_ref,
                  g1_ref, b1_ref, h_ref):
    y = yc_ref[...] + d_ref[...] * u_ref[...]
    gl = _gelu_tanh(y)
    so = gl * _sigmoid(jnp.dot(gl.astype(BF16), wglu_ref[...], preferred_element_type=F32))
    merged = (_sigmoid(gs_ref[...]) * jnp.dot(so.astype(BF16), wbs_ref[...], preferred_element_type=F32)
              + _sigmoid(ga_ref[...]) * jnp.dot(at_ref[...].astype(BF16), wba_ref[...], preferred_element_type=F32))
    mix = jnp.dot(merged.astype(BF16), wout_ref[...], preferred_element_type=F32)
    h_ref[...] = _layer_norm(DEEPNORM_ALPHA * x_ref[...] + mix, g1_ref[...], b1_ref[...])


def _merge(yc, z, gates, attn, x2d, d_skip, w_glu, w_bs, w_ba, w_out, ln_g, ln_b, tm):
    n, dm = x2d.shape
    aw = attn.shape[1]
    ssm_width = yc.shape[1]
    row = lambda w, cb=0: pl.BlockSpec((tm, w), lambda i: (i, cb))
    full = lambda a: pl.BlockSpec(a.shape, lambda i: (0, 0), pipeline_mode=pl.Buffered(1))
    return pl.pallas_call(
        _merge_kernel,
        grid=(n // tm,),
        in_specs=[row(ssm_width), row(ssm_width), row(dm, 0), row(dm, 1), row(aw), row(dm),
                  full(d_skip), full(w_glu), full(w_bs), full(w_ba), full(w_out), full(ln_g), full(ln_b)],
        out_specs=row(dm),
        out_shape=jax.ShapeDtypeStruct((n, dm), F32),
        compiler_params=_cparams(("parallel",)),
        name="merge",
    )(yc, z, gates, gates, attn, x2d, d_skip, w_glu, w_bs, w_ba, w_out, ln_g, ln_b)


def _order(x, y):
    if y is None:
        return x, None
    if x is None:
        return y, None
    (va, ta), (vb, tb) = x, y
    first = (va > vb) | ((va == vb) & (ta < tb))
    return ((jnp.where(first, va, vb), jnp.where(first, ta, tb)),
            (jnp.where(first, vb, va), jnp.where(first, tb, ta)))


def _bitonic_merge(xs):
    n = len(xs)
    if n == 1:
        return xs
    pairs = [_order(xs[i], xs[i + n // 2]) for i in range(n // 2)]
    return _bitonic_merge([p[0] for p in pairs]) + _bitonic_merge([p[1] for p in pairs])


def _sort_desc(xs):
    n = len(xs)
    if n == 1:
        return xs
    return _bitonic_merge(_sort_desc(xs[:n // 2]) + _sort_desc(xs[n // 2:])[::-1])


def _top_of_two(xs, ys):
    n = len(xs)
    return _bitonic_merge([_order(xs[i], ys[n - 1 - i])[0] for i in range(n)])


def _top_sorted(items, k):
    items = items + [None] * (-len(items) % k)
    group = lambda s: _sort_desc([it() if callable(it) else it for it in items[s:s + k]])
    run = group(0)
    for s in range(k, len(items), k):
        run = _top_of_two(run, group(s))
    return run


def _peer_topk_kernel(h_ref, wq_ref, k1_ref, k2_ref, a_ref, b_ref, g_ref, s_scr, o_scr, *, half, n_keys, tm):
    k = PEER_TOPK
    nj = tm // 128
    key_bits = n_keys.bit_length() - 1
    hi = lax.Precision.HIGHEST
    x = h_ref[...].astype(BF16)

    def half_top(keys_ref, qh):
        s = lax.dot_general(keys_ref[...], qh, NT_DIMS, precision=hi, preferred_element_type=F32)
        for kh in range(n_keys // 8):
            for j in range(nj):
                s_scr[pl.ds((kh * nj + j) * 8, 8), :] = s[kh * 8:(kh + 1) * 8, j * 128:(j + 1) * 128]
        def item(key):
            start = (key // 8) * nj * 8 + key % 8
            rows = pl.ds(start, nj, stride=8) if nj > 1 else pl.ds(start, 1)
            return lambda: (s_scr[rows, :], jnp.full((nj, 128), key, I32))

        return _top_sorted([item(key) for key in range(n_keys)], k)

    for hd in range(PEER_HEADS):
        q = jnp.dot(x, wq_ref[:, hd * 2 * half:(hd + 1) * 2 * half], preferred_element_type=F32)
        t1 = half_top(k1_ref, q[:, :half])
        t2 = half_top(k2_ref, q[:, half:])
        cands = []
        for r in range(k):
            for c in range(k):
                if (r + 1) * (c + 1) <= k:
                    tag = (t1[r][1] << key_bits) + t2[c][1] + ((r * k + c) << (2 * key_bits))
                    cands.append((t1[r][0] + t2[c][0], tag))
        top = _top_sorted(cands, k)
        ex = [jnp.exp(v - top[0][0]) for v, _ in top]
        tot = ex[0]
        for e in ex[1:]:
            tot = tot + e
        for i, (_, tag) in enumerate(top):
            row = pl.ds((hd * k + i) * nj, nj)
            o_scr[0, row, :] = ((tag >> key_bits) & (n_keys - 1)).astype(F32)
            o_scr[1, row, :] = (tag & (n_keys - 1)).astype(F32)
            o_scr[2, row, :] = ex[i] / tot
    ne = PEER_HEADS * k
    for j in range(nj):
        rows = pl.ds(j, ne, stride=nj) if nj > 1 else pl.ds(0, ne)
        tok = pl.ds(j * 128, 128)
        a_ref[tok, :] = o_scr[0, rows, :].T.astype(I32)
        b_ref[tok, :] = o_scr[1, rows, :].T.astype(I32)
        g_ref[tok, :] = o_scr[2, rows, :].T


def _peer_topk(h, wq_bf16, k1, k2, tm):
    n, dm = h.shape
    n_keys, half = k1.shape
    ne = PEER_HEADS * PEER_TOPK
    assert n_keys & (n_keys - 1) == 0 and tm % 128 == 0 and ne == 128
    assert (PEER_TOPK * PEER_TOPK) << (2 * (n_keys.bit_length() - 1)) < 2 ** 31
    full = lambda a: pl.BlockSpec(a.shape, lambda i: (0, 0), pipeline_mode=pl.Buffered(1))
    out = lambda: pl.BlockSpec((tm, ne), lambda i: (i, 0))
    return pl.pallas_call(
        functools.partial(_peer_topk_kernel, half=half, n_keys=n_keys, tm=tm),
        grid=(n // tm,),
        in_specs=[pl.BlockSpec((tm, dm), lambda i: (i, 0)), full(wq_bf16), full(k1), full(k2)],
        out_specs=[out(), out(), out()],
        out_shape=[jax.ShapeDtypeStruct((n, ne), I32), jax.ShapeDtypeStruct((n, ne), I32),
                   jax.ShapeDtypeStruct((n, ne), F32)],
        scratch_shapes=[pltpu.VMEM((n_keys * tm // 128, 128), F32), pltpu.VMEM((3, ne * tm // 128, 128), F32)],
        compiler_params=_cparams(("parallel",)),
        name="peer_topk",
    )(h, wq_bf16, k1, k2)


def _peer_dense_kernel(a_ref, b_ref, g_ref, h_ref, u_ref, v_ref, g2_ref, b2_ref, y_ref, w_scr, x_scr, acc_scr, p_scr,
                       *, tm, n_keys, rows_per_step):
    e_blk = pl.program_id(1)
    n_e = pl.num_programs(1) - 1
    half = tm // 2
    ne = a_ref.shape[1]
    u32 = jnp.uint32

    @pl.when(e_blk == 0)
    def _():
        x_scr[...] = h_ref[...].astype(BF16)
        acc_scr[...] = jnp.zeros_like(acc_scr)
        p_scr[1] = jnp.zeros(p_scr.shape[1:], BF16)
        key_iota = lax.broadcasted_iota(I32, (n_keys, ne), 0)

        def gate_bits(t):
            a_row = a_ref[pl.ds(t, 1), :]
            b_row = b_ref[pl.ds(t, 1), :]
            g_row = g_ref[pl.ds(t, 1), :]
            sel_a = jnp.where(key_iota == a_row, 1.0, 0.0).astype(BF16)
            sel_b = jnp.where(key_iota == b_row, g_row, 0.0).astype(BF16)
            w = lax.dot_general(sel_a, sel_b, NT_DIMS, preferred_element_type=F32)
            return lax.bitcast_convert_type(w.astype(BF16).astype(F32), u32)

        def build(t, carry):
            word = gate_bits(t) | (gate_bits(t + half) >> 16)
            w_scr[pl.ds(pl.multiple_of(t * n_keys, n_keys), n_keys), :] = word
            return carry

        lax.fori_loop(0, half, build, 0, unroll=4)

    slot = e_blk % 2
    acc_scr[...] += jnp.dot(p_scr[1 - slot], v_ref[...], preferred_element_type=F32)

    z = lax.dot_general(x_scr[...], u_ref[...], NT_DIMS, preferred_element_type=F32)
    first_row = jnp.minimum(e_blk, n_e - 1) * rows_per_step
    for c in range(rows_per_step):
        word = w_scr[pl.ds(first_row + c, half, stride=n_keys), :]
        w_a = lax.bitcast_convert_type(word & u32(0xFFFF0000), F32)
        w_b = lax.bitcast_convert_type(word << 16, F32)
        cols = slice(c * n_keys, (c + 1) * n_keys)
        p_scr[slot, 0:half, cols] = (w_a * _gelu_tanh(z[0:half, cols])).astype(BF16)
        p_scr[slot, half:tm, cols] = (w_b * _gelu_tanh(z[half:tm, cols])).astype(BF16)

    @pl.when(e_blk == n_e)
    def _():
        y_ref[...] = _layer_norm(DEEPNORM_ALPHA * h_ref[...] + acc_scr[...], g2_ref[...], b2_ref[...])


def _peer_dense(a, b, g, h, u_bf16, v_bf16, ln_g, ln_b, n_keys, tm, rows_per_step):
    n, dm = h.shape
    ne = a.shape[1]
    eb = rows_per_step * n_keys
    n_e = n_keys // rows_per_step
    tok = lambda w: pl.BlockSpec((tm, w), lambda i, e: (i, 0))
    full = lambda arr: pl.BlockSpec(arr.shape, lambda i, e: (0, 0))
    return pl.pallas_call(
        functools.partial(_peer_dense_kernel, tm=tm, n_keys=n_keys, rows_per_step=rows_per_step),
        grid=(n // tm, n_e + 1),
        in_specs=[tok(ne), tok(ne), tok(ne), tok(dm),
                  pl.BlockSpec((eb, dm), lambda i, e: (jnp.minimum(e, n_e - 1), 0)),
                  pl.BlockSpec((eb, dm), lambda i, e: (jnp.maximum(e - 1, 0), 0)),
                  full(ln_g), full(ln_b)],
        out_specs=tok(dm),
        out_shape=jax.ShapeDtypeStruct((n, dm), F32),
        scratch_shapes=[pltpu.VMEM((tm // 2 * n_keys, n_keys), jnp.uint32), pltpu.VMEM((tm, dm), BF16),
                        pltpu.VMEM((tm, dm), F32), pltpu.VMEM((2, tm, eb), BF16)],
        compiler_params=_cparams(("parallel", "arbitrary")),
        name="peer_experts",
    )(a, b, g, h, u_bf16, v_bf16, ln_g, ln_b)


def _layer(x, caches, h0, wts, ssm_mats):
    (w_in, ssm_d, w_glu, w_bs, w_ba, w_out, ln1_g, ln1_b, wq, k1, k2, pu, pv, ln2_g, ln2_b, slopes) = wts
    m, em, fm, am = ssm_mats
    bsz, seq, dm = x.shape
    n = bsz * seq
    ssm_width = ssm_d.shape[1]
    in_width = w_in.shape[1]
    qkv_width = (in_width - ssm_width - 2 * dm) // 3
    kv_half = HEADS_PER_GROUP * HEAD_DIM
    k_off, v_off = ssm_width + qkv_width, ssm_width + 2 * qkv_width
    x2d = x.reshape(n, dm)

    gate_off = ssm_width + 3 * qkv_width
    z = _proj(x2d, w_in[:, :gate_off], _pick_tile(n, 1024), _pick_tile(gate_off, 512))
    gates = _proj(x2d, w_in[:, gate_off:], _pick_tile(n, 1024), _pick_tile(2 * dm, 512))
    z3 = z.reshape(bsz, seq, gate_off)

    new_kv = []
    if caches is None:
        qb = seq // max(DILATIONS)
        attn = _attn_prompt(z3, slopes, ssm_width, qkv_width, qb).reshape(n, kv_half)
        for gi, d in enumerate(DILATIONS):
            keep = min(qb * d, seq)
            kg = z3[:, seq - keep:, k_off + gi * kv_half:k_off + (gi + 1) * kv_half]
            vg = z3[:, seq - keep:, v_off + gi * kv_half:v_off + (gi + 1) * kv_half]
            new_kv.append(jnp.stack([kg, vg], axis=2).reshape(bsz, keep, 2, HEADS_PER_GROUP, HEAD_DIM))
    else:
        q = z3[:, :, ssm_width:k_off]
        k = z3[:, :, k_off:v_off]
        v = z3[:, :, v_off:v_off + qkv_width]
        attn = _attn_sample(slopes, q, k, v, list(caches)).reshape(n, kv_half)
        for gi in range(len(DILATIONS)):
            kg = k[:, :, gi * kv_half:(gi + 1) * kv_half]
            vg = v[:, :, gi * kv_half:(gi + 1) * kv_half]
            new_kv.append(jnp.stack([kg, vg], axis=2).reshape(bsz, seq, 2, HEADS_PER_GROUP, HEAD_DIM))

    ns = m.shape[0]
    p_state = em.shape[2] // (2 * SSM_PACK)
    lanes = ssm_width // ns
    u3 = z3[:, :, :ssm_width]
    exact_state = h0 is not None
    u2 = _to_chunk_rows(u3 if exact_state else u3.astype(BF16), SSM_CHUNK, lanes)
    h0s = (_state_to_slabs(h0, ns) if exact_state
           else jnp.zeros((ns, bsz, 2 * SSM_PACK * p_state), F32))
    y2, hfin = _ssm(u2, m, em, fm, am, h0s, bsz, exact_state)
    yc = _from_chunk_rows(y2, bsz, SSM_CHUNK, lanes)
    ssm_state = _state_from_slabs(hfin, p_state)

    h = _merge(yc, z, gates, attn, x2d, ssm_d, w_glu, w_bs, w_ba, w_out, ln1_g, ln1_b, _pick_tile(n, 256))
    a, b, g = _peer_topk(h, wq, k1, k2, _pick_tile(n, 1024))
    y = _peer_dense(a, b, g, h, pu, pv, ln2_g, ln2_b, k1.shape[0], _pick_tile(n, 512), 4)
    return y.reshape(bsz, seq, dm), new_kv, ssm_state


def kernel(x_prompt, x_sample, cache_kv_w128, cache_kv_w512, cache_kv_w2048, state_ssm, w_in, ssm_log_dt, ssm_a_re, ssm_a_im, ssm_b_re, ssm_b_im, ssm_c_re, ssm_c_im, ssm_d, ssm_w_glu, w_branch_ssm, w_branch_attn, w_out, ln1_g, ln1_b, peer_w_q, peer_sub_keys_1, peer_sub_keys_2, peer_u, peer_v, ln2_g, ln2_b):
    caches = (cache_kv_w128, cache_kv_w512, cache_kv_w2048)
    seq = x_prompt.shape[1]
    assert x_sample.shape[1] == SSM_CHUNK and seq % (SSM_CHUNK * max(DILATIONS)) == 0
    assert all(c.shape[1] == (seq // max(DILATIONS)) * d and c.shape[1] <= PAST_LEN
               for c, d in zip(caches, DILATIONS))
    n_heads = len(DILATIONS) * HEADS_PER_GROUP
    slopes = 2.0 ** (-8.0 * jnp.arange(1, n_heads + 1, dtype=F32) / n_heads)
    row = lambda a: a.reshape(1, -1).astype(F32)
    wts = (w_in.astype(BF16), row(ssm_d), ssm_w_glu.astype(BF16), w_branch_ssm.astype(BF16),
           w_branch_attn.astype(BF16), w_out.astype(BF16), row(ln1_g), row(ln1_b), peer_w_q.astype(BF16),
           peer_sub_keys_1.astype(F32), peer_sub_keys_2.astype(F32), peer_u.astype(BF16), peer_v.astype(BF16),
           row(ln2_g), row(ln2_b), slopes)
    ssm_mats = _ssm_layout(*_ssm_prep(ssm_log_dt, ssm_a_re, ssm_a_im, ssm_b_re, ssm_b_im, ssm_c_re, ssm_c_im,
                                      SSM_CHUNK), SSM_CHUNK)
    y_p, kv_p, ssm_p = _layer(x_prompt, None, None, wts, ssm_mats)
    y_s, kv_s, ssm_s = _layer(x_sample, caches, state_ssm, wts, ssm_mats)
    return (y_p, y_s, kv_p[0], kv_p[1], kv_p[2], ssm_p, kv_s[0], kv_s[1], kv_s[2], ssm_s)
```

```python
import functools
import math

import jax
import jax.numpy as jnp
from jax import lax
from jax.experimental import pallas as pl
from jax.experimental.pallas import tpu as pltpu

F32 = jnp.float32
BF16 = jnp.bfloat16
I32 = jnp.int32

HEAD_DIM = 128
HEADS_PER_GROUP = 4
DILATIONS = (1, 4, 16)
PAST_LEN = 16384
PEER_HEADS = 8
PEER_TOPK = 16
DEPTH = 1
DEEPNORM_ALPHA = (2.0 * DEPTH) ** 0.25
LN_EPS = 1e-5
SSM_CHUNK = 8
SSM_PACK = 8
NEG = -1e30
VMEM_LIMIT = 56 * 1024 * 1024

NT_DIMS = (((1,), (1,)), ((), ()))


def _cparams(sem):
    return pltpu.CompilerParams(dimension_semantics=sem, vmem_limit_bytes=VMEM_LIMIT)


def _gelu_tanh(x):
    return 0.5 * x * (1.0 + jnp.tanh(math.sqrt(2.0 / math.pi) * (x + 0.044715 * (x * x * x))))


def _sigmoid(x):
    return 1.0 / (1.0 + jnp.exp(-x))


def _layer_norm(x, g, b):
    mu = jnp.mean(x, axis=-1, keepdims=True)
    xc = x - mu
    var = jnp.mean(xc * xc, axis=-1, keepdims=True)
    return xc * lax.rsqrt(var + LN_EPS) * g + b


def _pick_tile(n, pref):
    t = min(n, pref)
    while n % t:
        t //= 2
    return t


def _proj_kernel(x_ref, w_ref, o_ref):
    o_ref[...] = jnp.dot(x_ref[...].astype(BF16), w_ref[...], preferred_element_type=F32)


def _proj(x2d, w_bf16, tm, tn):
    n, d = x2d.shape
    nw = w_bf16.shape[1]
    return pl.pallas_call(
        _proj_kernel,
        grid=(n // tm, nw // tn),
        in_specs=[pl.BlockSpec((tm, d), lambda i, j: (i, 0)),
                  pl.BlockSpec((d, tn), lambda i, j: (0, j))],
        out_specs=pl.BlockSpec((tm, tn), lambda i, j: (i, j)),
        out_shape=jax.ShapeDtypeStruct((n, nw), F32),
        compiler_params=_cparams(("parallel", "arbitrary")),
        name="in_proj",
    )(x2d, w_bf16)


def _rows(start, size, stride):
    return pl.ds(start, size) if stride == 1 else pl.ds(start, size, stride=stride)


def _attn_prompt_kernel(slopes_ref, q_ref, k_ref, v_ref, o_ref, m_ref, l_ref, *, seq, qb, scale):
    hs = pl.program_id(1)
    g = pl.program_id(2)
    n_groups = len(DILATIONS)
    for gi, d in enumerate(DILATIONS):
        @pl.when(g == gi)
        def _(gi=gi, d=d):
            slope = slopes_ref[gi * HEADS_PER_GROUP + hs] * float(d)
            for r in range(d):
                for i in range(seq // (d * qb)):
                    rq = _rows(r + i * qb * d, qb, d)
                    q = q_ref[0, rq, :].astype(BF16)
                    if i == 0:
                        rk, nk, koff = rq, qb, 0
                    else:
                        rk, nk, koff = _rows(r + (i - 1) * qb * d, 2 * qb, d), 2 * qb, qb
                    kk = k_ref[0, rk, :].astype(BF16)
                    vv = v_ref[0, rk, :].astype(BF16)
                    s = lax.dot_general(q, kk, NT_DIMS, preferred_element_type=F32) * scale
                    row = lax.broadcasted_iota(I32, (qb, nk), 0)
                    col = lax.broadcasted_iota(I32, (qb, nk), 1)
                    j = row + koff - col
                    s = s - slope * j.astype(F32)
                    s = jnp.where((j >= 0) & (j <= qb), s, NEG)
                    m_blk = jnp.max(s, axis=1, keepdims=True)
                    p = jnp.exp(s - m_blk)
                    l_blk = jnp.sum(p, axis=1, keepdims=True)
                    o_blk = jnp.dot(p.astype(BF16), vv, preferred_element_type=F32)
                    m_b = jnp.broadcast_to(m_blk, (qb, HEAD_DIM))
                    l_b = jnp.broadcast_to(l_blk, (qb, HEAD_DIM))
                    if gi == 0:
                        m_new, l_new, acc = m_b, l_b, o_blk
                    else:
                        m_old = m_ref[rq, :]
                        m_new = jnp.maximum(m_old, m_b)
                        a_old = jnp.exp(m_old - m_new)
                        a_blk = jnp.exp(m_b - m_new)
                        l_new = a_old * l_ref[rq, :] + a_blk * l_b
                        acc = a_old * o_ref[0, rq, :] + a_blk * o_blk
                    if gi == n_groups - 1:
                        o_ref[0, rq, :] = acc / l_new
                    else:
                        m_ref[rq, :] = m_new
                        l_ref[rq, :] = l_new
                        o_ref[0, rq, :] = acc


def _attn_prompt(z3, slopes, ssm_width, qkv_width, qb):
    bsz, seq, _ = z3.shape
    blk = lambda base: (lambda b, hs, g: (b, 0, base + g * HEADS_PER_GROUP + hs))
    qc, kc, vc = (ssm_width // HEAD_DIM, (ssm_width + qkv_width) // HEAD_DIM,
                  (ssm_width + 2 * qkv_width) // HEAD_DIM)
    return pl.pallas_call(
        functools.partial(_attn_prompt_kernel, seq=seq, qb=qb, scale=HEAD_DIM ** -0.5),
        grid=(bsz, HEADS_PER_GROUP, len(DILATIONS)),
        in_specs=[pl.BlockSpec(memory_space=pltpu.SMEM),
                  pl.BlockSpec((1, seq, HEAD_DIM), blk(qc)),
                  pl.BlockSpec((1, seq, HEAD_DIM), blk(kc)),
                  pl.BlockSpec((1, seq, HEAD_DIM), blk(vc))],
        out_specs=pl.BlockSpec((1, seq, HEAD_DIM), lambda b, hs, g: (b, 0, hs)),
        out_shape=jax.ShapeDtypeStruct((bsz, seq, HEADS_PER_GROUP * HEAD_DIM), F32),
        scratch_shapes=[pltpu.VMEM((seq, HEAD_DIM), F32), pltpu.VMEM((seq, HEAD_DIM), F32)],
        compiler_params=_cparams(("parallel", "parallel", "arbitrary")),
        name="attn_prompt",
    )(slopes, z3, z3, z3)


def _attn_sample_kernel(slopes_ref, q_ref, k_ref, v_ref, c0_ref, c1_ref, c2_ref, o_ref, *, n_new, scale):
    caches = (c0_ref, c1_ref, c2_ref)
    for hs in range(HEADS_PER_GROUP):
        pieces = []
        for gi, d in enumerate(DILATIONS):
            c_ref = caches[gi]
            win = c_ref.shape[1]
            hcol = (gi * HEADS_PER_GROUP + hs) * HEAD_DIM
            slope = slopes_ref[gi * HEADS_PER_GROUP + hs]
            q = q_ref[0, :, hcol:hcol + HEAD_DIM].astype(BF16)
            k_new = k_ref[0, :, hcol:hcol + HEAD_DIM].astype(BF16)
            v_new = v_ref[0, :, hcol:hcol + HEAD_DIM].astype(BF16)
            k_old = c_ref[0, :, 0, hs, :].astype(BF16)
            v_old = c_ref[0, :, 1, hs, :].astype(BF16)
            for kk, vv, nk, base in ((k_old, v_old, win, win), (k_new, v_new, n_new, 0)):
                s = lax.dot_general(q, kk, NT_DIMS, preferred_element_type=F32) * scale
                row = lax.broadcasted_iota(I32, (n_new, nk), 0)
                col = lax.broadcasted_iota(I32, (n_new, nk), 1)
                offs = base + row - col
                valid = (offs >= 0) & (offs <= win) & ((offs & (d - 1)) == 0)
                s = jnp.where(valid, s - slope * offs.astype(F32), NEG)
                pieces.append((s, vv))
        m = pieces[0][0].max(axis=1, keepdims=True)
        for s, _ in pieces[1:]:
            m = jnp.maximum(m, s.max(axis=1, keepdims=True))
        l = jnp.zeros((n_new, 1), F32)
        acc = jnp.zeros((n_new, HEAD_DIM), F32)
        for s, vv in pieces:
            p = jnp.exp(s - m)
            l = l + p.sum(axis=1, keepdims=True)
            acc = acc + jnp.dot(p.astype(BF16), vv, preferred_element_type=F32)
        o_ref[0, :, hs * HEAD_DIM:(hs + 1) * HEAD_DIM] = acc / l


def _attn_sample(slopes, q, k, v, caches):
    bsz, n_new, qkv_width = q.shape
    assert all(d & (d - 1) == 0 for d in DILATIONS)
    row_spec = lambda a: pl.BlockSpec((1,) + a.shape[1:], lambda b: (b,) + (0,) * (a.ndim - 1))
    return pl.pallas_call(
        functools.partial(_attn_sample_kernel, n_new=n_new, scale=HEAD_DIM ** -0.5),
        grid=(bsz,),
        in_specs=[pl.BlockSpec(memory_space=pltpu.SMEM), row_spec(q), row_spec(k), row_spec(v)]
                 + [row_spec(c) for c in caches],
        out_specs=pl.BlockSpec((1, n_new, HEADS_PER_GROUP * HEAD_DIM), lambda b: (b, 0, 0)),
        out_shape=jax.ShapeDtypeStruct((bsz, n_new, HEADS_PER_GROUP * HEAD_DIM), F32),
        compiler_params=_cparams(("parallel",)),
        name="attn_sample",
    )(slopes, q, k, v, *caches)


def _ssm_prep_kernel(ldt_ref, ar_r_ref, ai_r_ref, ar_c_ref, ai_c_ref, br_ref, bi_ref, cr_ref, ci_ref,
                     kt_ref, e_ref, f_ref, at_ref, *, chunk):
    dt = jnp.exp(ldt_ref[0])

    def discretise(ar, ai):
        mag = jnp.exp(ar * dt)
        return mag * jnp.cos(ai * dt), mag * jnp.sin(ai * dt)

    ar_r, ai_r = ar_r_ref[0], ai_r_ref[0]
    ar_c, ai_c = ar_c_ref[0], ai_c_ref[0]
    abr_r, abi_r = discretise(ar_r, ai_r)
    abr_c, abi_c = discretise(ar_c, ai_c)
    den = ar_c * ar_c + ai_c * ai_c
    f_re = ((abr_c - 1.0) * ar_c + abi_c * ai_c) / den
    f_im = (abi_c * ar_c - (abr_c - 1.0) * ai_c) / den
    b_re, b_im = br_ref[0], bi_ref[0]
    bb_re = f_re * b_re - f_im * b_im
    bb_im = f_re * b_im + f_im * b_re
    c_re, c_im = cr_ref[0], ci_ref[0]

    def powers(re1, im1):
        pr, pi = [jnp.ones_like(re1)], [jnp.zeros_like(re1)]
        for _ in range(chunk):
            pr.append(pr[-1] * re1 - pi[-1] * im1)
            pi.append(pr[-2] * im1 + pi[-1] * re1)
        return pr, pi

    pr_r, pi_r = powers(abr_r, abi_r)
    pr_c, pi_c = powers(abr_c, abi_c)
    hi = lax.Precision.HIGHEST
    for t in range(chunk):
        cp_re = c_re * pr_r[t] - c_im * pi_r[t]
        cp_im = c_re * pi_r[t] + c_im * pr_r[t]
        kt_ref[0, t] = (jnp.dot(cp_re, bb_re, precision=hi, preferred_element_type=F32)
                        - jnp.dot(cp_im, bb_im, precision=hi, preferred_element_type=F32))
        qr, qi = pr_c[chunk - 1 - t], pi_c[chunk - 1 - t]
        e_ref[0, t, 0] = qr * bb_re - qi * bb_im
        e_ref[0, t, 1] = qr * bb_im + qi * bb_re
        f_ref[0, t, 0] = c_re * pr_r[t + 1] - c_im * pi_r[t + 1]
        f_ref[0, t, 1] = -(c_re * pi_r[t + 1] + c_im * pr_r[t + 1])
    at_ref[0, 0:1, :] = pr_r[chunk]
    at_ref[0, 1:2, :] = pi_r[chunk]


def _ssm_prep(log_dt, a_re, a_im, b_re, b_im, c_re, c_im, chunk):
    g, p = a_re.shape
    h = b_re.shape[2]
    g3 = lambda *s: pl.BlockSpec((1,) + s, lambda i: (i,) + (0,) * len(s))
    return pl.pallas_call(
        functools.partial(_ssm_prep_kernel, chunk=chunk),
        grid=(g,),
        in_specs=[g3(1, 1), g3(1, p), g3(1, p), g3(p, 1), g3(p, 1), g3(p, h), g3(p, h), g3(h, p), g3(h, p)],
        out_specs=[g3(chunk, h, h), g3(chunk, 2, p, h), g3(chunk, 2, h, p), g3(2, p)],
        out_shape=[jax.ShapeDtypeStruct((g, chunk, h, h), F32),
                   jax.ShapeDtypeStruct((g, chunk, 2, p, h), F32),
                   jax.ShapeDtypeStruct((g, chunk, 2, h, p), F32),
                   jax.ShapeDtypeStruct((g, 2, p), F32)],
        compiler_params=_cparams(("parallel",)),
        name="ssm_prep",
    )(log_dt.reshape(g, 1, 1), a_re.reshape(g, 1, p), a_im.reshape(g, 1, p),
      a_re.reshape(g, p, 1), a_im.reshape(g, p, 1), b_re, b_im, c_re, c_im)


def _ssm_layout(kt, e, f, at, chunk):
    g, _, h, _ = kt.shape
    p = at.shape[2]
    ns = g // SSM_PACK
    eye = jnp.eye(SSM_PACK, dtype=F32)
    lag = jnp.arange(chunk)[None, :] - jnp.arange(chunk)[:, None]
    ktoe = jnp.where((lag >= 0)[None, :, :, None, None],
                     kt[:, jnp.clip(lag, 0, chunk - 1)], 0.0)
    ktoe = ktoe.reshape(ns, SSM_PACK, chunk, chunk, h, h)
    m = jnp.einsum('ogijkh,gq->oighjqk', ktoe, eye).reshape(ns, chunk * SSM_PACK * h, chunk * SSM_PACK * h)
    e6 = e.reshape(ns, SSM_PACK, chunk, 2, p, h)
    em = jnp.einsum('ogirph,gq->oighrqp', e6, eye).reshape(ns, chunk * SSM_PACK * h, 2 * SSM_PACK * p)
    f6 = f.reshape(ns, SSM_PACK, chunk, 2, h, p)
    fm = jnp.einsum('ogirhp,gq->orgpiqh', f6, eye).reshape(ns, 2 * SSM_PACK * p, chunk * SSM_PACK * h)
    am = at.reshape(ns, SSM_PACK, 2, p).transpose(0, 2, 1, 3).reshape(ns, 2, SSM_PACK * p)
    return m.astype(BF16), em, fm.astype(BF16), am


def _ssm_kernel(u_ref, m_ref, e_ref, f_ref, a_ref, h0_ref, y_ref, hfin_ref, h_scr, s_scr, hp_scr,
                *, bt, n_chunks, exact_state):
    half = a_ref.shape[2]

    @pl.when(pl.program_id(1) == 0)
    def _():
        h_scr[...] = h0_ref[0]

    u = u_ref[...]
    if exact_state:
        s_scr[...] = jnp.dot(u, e_ref[0], precision=lax.Precision.HIGHEST, preferred_element_type=F32)
    else:
        s_scr[...] = jnp.dot(u.astype(BF16), e_ref[0].astype(BF16), preferred_element_type=F32)
    a_re = jnp.broadcast_to(a_ref[0, 0:1, :], (bt, half))
    a_im = jnp.broadcast_to(a_ref[0, 1:2, :], (bt, half))

    def step(c, carry):
        h_re, h_im = carry
        rows = pl.ds(pl.multiple_of(c * bt, bt), bt)
        hp_scr[rows, 0:half] = h_re
        hp_scr[rows, half:2 * half] = h_im
        n_re = a_re * h_re - a_im * h_im + s_scr[rows, 0:half]
        n_im = a_re * h_im + a_im * h_re + s_scr[rows, half:2 * half]
        return n_re, n_im

    h_re, h_im = lax.fori_loop(0, n_chunks, step, (h_scr[:, 0:half], h_scr[:, half:2 * half]))
    h_scr[:, 0:half] = h_re
    h_scr[:, half:2 * half] = h_im
    hfin_ref[0] = h_scr[...]
    y_ref[...] = (jnp.dot(u.astype(BF16), m_ref[0], preferred_element_type=F32)
                  + jnp.dot(hp_scr[...].astype(BF16), f_ref[0], preferred_element_type=F32)).astype(y_ref.dtype)


def _ssm(u2, m, em, fm, am, h0, bt, exact_state):
    rows, _ = u2.shape
    ns, sw, _ = m.shape
    st = em.shape[2]
    tr = _pick_tile(rows // bt, 64) * bt
    slab = lambda i, r: (i, 0, 0)
    return pl.pallas_call(
        functools.partial(_ssm_kernel, bt=bt, n_chunks=tr // bt, exact_state=exact_state),
        grid=(ns, rows // tr),
        in_specs=[pl.BlockSpec((tr, sw), lambda i, r: (r, i)),
                  pl.BlockSpec((1, sw, sw), slab), pl.BlockSpec((1, sw, st), slab),
                  pl.BlockSpec((1, st, sw), slab), pl.BlockSpec((1, 2, st // 2), slab),
                  pl.BlockSpec((1, bt, st), slab)],
        out_specs=[pl.BlockSpec((tr, sw), lambda i, r: (r, i)), pl.BlockSpec((1, bt, st), slab)],
        out_shape=[jax.ShapeDtypeStruct((rows, ns * sw), BF16), jax.ShapeDtypeStruct((ns, bt, st), F32)],
        scratch_shapes=[pltpu.VMEM((bt, st), F32), pltpu.VMEM((tr, st), F32), pltpu.VMEM((tr, st), F32)],
        compiler_params=_cparams(("parallel", "arbitrary")),
        name="ssm_scan",
    )(u2, m, em, fm, am, h0)


def _to_chunk_rows(u, chunk, lanes):
    b, l, w = u.shape
    c = l // chunk
    return u.reshape(b, c, chunk, w // lanes, lanes).transpose(1, 0, 3, 2, 4).reshape(c * b, chunk * w)


def _from_chunk_rows(y2, b, chunk, lanes):
    rows, cw = y2.shape
    c, w = rows // b, cw // chunk
    return y2.reshape(c, b, w // lanes, chunk, lanes).transpose(1, 0, 3, 2, 4).reshape(b * c * chunk, w)


def _state_to_slabs(h, ns):
    b, g, p, _ = h.shape
    return h.reshape(b, ns, SSM_PACK, p, 2).transpose(1, 0, 4, 2, 3).reshape(ns, b, 2 * SSM_PACK * p)


def _state_from_slabs(hs, p):
    ns, b, _ = hs.shape
    return hs.reshape(ns, b, 2, SSM_PACK, p).transpose(1, 0, 3, 4, 2).reshape(b, ns * SSM_PACK, p, 2)


def _merge_kernel(yc_ref, u_ref, gs_ref, ga_ref, at_ref, x_ref, d_ref, wglu_ref, wbs_ref, wba_ref, wout_ref,
                  g1_ref, b1_ref, h_ref):
    y = yc_ref[...].astype(F32) + d_ref[...] * u_ref[...]
    gl = _gelu_tanh(y)
    so = gl * _sigmoid(jnp.dot(gl.astype(BF16), wglu_ref[...], preferred_element_type=F32))
    merged = (_sigmoid(gs_ref[...]) * jnp.dot(so.astype(BF16), wbs_ref[...], preferred_element_type=F32)
              + _sigmoid(ga_ref[...]) * jnp.dot(at_ref[...].astype(BF16), wba_ref[...], preferred_element_type=F32))
    mix = jnp.dot(merged.astype(BF16), wout_ref[...], preferred_element_type=F32)
    h_ref[...] = _layer_norm(DEEPNORM_ALPHA * x_ref[...] + mix, g1_ref[...], b1_ref[...])


def _merge(yc, z, gates, attn, x2d, d_skip, w_glu, w_bs, w_ba, w_out, ln_g, ln_b, tm):
    n, dm = x2d.shape
    aw = attn.shape[1]
    ssm_width = yc.shape[1]
    row = lambda w, cb=0: pl.BlockSpec((tm, w), lambda i: (i, cb))
    full = lambda a: pl.BlockSpec(a.shape, lambda i: (0, 0), pipeline_mode=pl.Buffered(1))
    return pl.pallas_call(
        _merge_kernel,
        grid=(n // tm,),
        in_specs=[row(ssm_width), row(ssm_width), row(dm, 0), row(dm, 1), row(aw), row(dm),
                  full(d_skip), full(w_glu), full(w_bs), full(w_ba), full(w_out), full(ln_g), full(ln_b)],
        out_specs=row(dm),
        out_shape=jax.ShapeDtypeStruct((n, dm), F32),
        compiler_params=_cparams(("parallel",)),
        name="merge",
    )(yc, z, gates, gates, attn, x2d, d_skip, w_glu, w_bs, w_ba, w_out, ln_g, ln_b)


def _order(x, y):
    if y is None:
        return x, None
    if x is None:
        return y, None
    (va, ta), (vb, tb) = x, y
    first = (va > vb) | ((va == vb) & (ta < tb))
    return ((jnp.where(first, va, vb), jnp.where(first, ta, tb)),
            (jnp.where(first, vb, va), jnp.where(first, tb, ta)))


def _bitonic_merge(xs):
    n = len(xs)
    if n == 1:
        return xs
    pairs = [_order(xs[i], xs[i + n // 2]) for i in range(n // 2)]
    return _bitonic_merge([p[0] for p in pairs]) + _bitonic_merge([p[1] for p in pairs])


def _sort_desc(xs):
    n = len(xs)
    if n == 1:
        return xs
    return _bitonic_merge(_sort_desc(xs[:n // 2]) + _sort_desc(xs[n // 2:])[::-1])


def _top_of_two(xs, ys):
    n = len(xs)
    return _bitonic_merge([_order(xs[i], ys[n - 1 - i])[0] for i in range(n)])


def _top_sorted(items, k):
    items = items + [None] * (-len(items) % k)
    group = lambda s: _sort_desc([it() if callable(it) else it for it in items[s:s + k]])
    run = group(0)
    for s in range(k, len(items), k):
        run = _top_of_two(run, group(s))
    return run


def _peer_topk_kernel(h_ref, wq_ref, k1_ref, k2_ref, a_ref, b_ref, g_ref, s_scr, o_scr, *, half, n_keys, tm):
    k = PEER_TOPK
    nj = tm // 128
    key_bits = n_keys.bit_length() - 1
    hi = lax.Precision.HIGHEST
    x = h_ref[...].astype(BF16)

    def half_top(keys_ref, qh):
        s = lax.dot_general(keys_ref[...], qh, NT_DIMS, precision=hi, preferred_element_type=F32)
        for kh in range(n_keys // 8):
            for j in range(nj):
                s_scr[pl.ds((kh * nj + j) * 8, 8), :] = s[kh * 8:(kh + 1) * 8, j * 128:(j + 1) * 128]
        def item(key):
            start = (key // 8) * nj * 8 + key % 8
            rows = pl.ds(start, nj, stride=8) if nj > 1 else pl.ds(start, 1)
            return lambda: (s_scr[rows, :], jnp.full((nj, 128), key, I32))

        return _top_sorted([item(key) for key in range(n_keys)], k)

    for hd in range(PEER_HEADS):
        q = jnp.dot(x, wq_ref[:, hd * 2 * half:(hd + 1) * 2 * half], preferred_element_type=F32)
        t1 = half_top(k1_ref, q[:, :half])
        t2 = half_top(k2_ref, q[:, half:])
        cands = []
        for r in range(k):
            for c in range(k):
                if (r + 1) * (c + 1) <= k:
                    tag = (t1[r][1] << key_bits) + t2[c][1] + ((r * k + c) << (2 * key_bits))
                    cands.append((t1[r][0] + t2[c][0], tag))
        top = _top_sorted(cands, k)
        ex = [jnp.exp(v - top[0][0]) for v, _ in top]
        tot = ex[0]
        for e in ex[1:]:
            tot = tot + e
        for i, (_, tag) in enumerate(top):
            row = pl.ds((hd * k + i) * nj, nj)
            o_scr[0, row, :] = ((tag >> key_bits) & (n_keys - 1)).astype(F32)
            o_scr[1, row, :] = (tag & (n_keys - 1)).astype(F32)
            o_scr[2, row, :] = ex[i] / tot
    ne = PEER_HEADS * k
    for j in range(nj):
        rows = pl.ds(j, ne, stride=nj) if nj > 1 else pl.ds(0, ne)
        tok = pl.ds(j * 128, 128)
        a_ref[tok, :] = o_scr[0, rows, :].T.astype(I32)
        b_ref[tok, :] = o_scr[1, rows, :].T.astype(I32)
        g_ref[tok, :] = o_scr[2, rows, :].T


def _peer_topk(h, wq_bf16, k1, k2, tm):
    n, dm = h.shape
    n_keys, half = k1.shape
    ne = PEER_HEADS * PEER_TOPK
    assert n_keys & (n_keys - 1) == 0 and tm % 128 == 0 and ne == 128
    assert (PEER_TOPK * PEER_TOPK) << (2 * (n_keys.bit_length() - 1)) < 2 ** 31
    full = lambda a: pl.BlockSpec(a.shape, lambda i: (0, 0), pipeline_mode=pl.Buffered(1))
    out = lambda: pl.BlockSpec((tm, ne), lambda i: (i, 0))
    return pl.pallas_call(
        functools.partial(_peer_topk_kernel, half=half, n_keys=n_keys, tm=tm),
        grid=(n // tm,),
        in_specs=[pl.BlockSpec((tm, dm), lambda i: (i, 0)), full(wq_bf16), full(k1), full(k2)],
        out_specs=[out(), out(), out()],
        out_shape=[jax.ShapeDtypeStruct((n, ne), I32), jax.ShapeDtypeStruct((n, ne), I32),
                   jax.ShapeDtypeStruct((n, ne), F32)],
        scratch_shapes=[pltpu.VMEM((n_keys * tm // 128, 128), F32), pltpu.VMEM((3, ne * tm // 128, 128), F32)],
        compiler_params=_cparams(("parallel",)),
        name="peer_topk",
    )(h, wq_bf16, k1, k2)


def _peer_dense_kernel(a_ref, b_ref, g_ref, h_ref, u_ref, v_ref, g2_ref, b2_ref, y_ref, w_scr, x_scr, acc_scr, p_scr,
                       *, tm, n_keys, rows_per_step):
    e_blk = pl.program_id(1)
    n_e = pl.num_programs(1) - 1
    half = tm // 2
    ne = a_ref.shape[1]
    u32 = jnp.uint32

    @pl.when(e_blk == 0)
    def _():
        x_scr[...] = h_ref[...].astype(BF16)
        acc_scr[...] = jnp.zeros_like(acc_scr)
        p_scr[1] = jnp.zeros(p_scr.shape[1:], BF16)
        key_iota = lax.broadcasted_iota(I32, (n_keys, ne), 0)

        def gate_bits(t):
            a_row = a_ref[pl.ds(t, 1), :]
            b_row = b_ref[pl.ds(t, 1), :]
            g_row = g_ref[pl.ds(t, 1), :]
            sel_a = jnp.where(key_iota == a_row, 1.0, 0.0).astype(BF16)
            sel_b = jnp.where(key_iota == b_row, g_row, 0.0).astype(BF16)
            w = lax.dot_general(sel_a, sel_b, NT_DIMS, preferred_element_type=F32)
            return lax.bitcast_convert_type(w.astype(BF16).astype(F32), u32)

        def build(t, carry):
            word = gate_bits(t) | (gate_bits(t + half) >> 16)
            w_scr[pl.ds(pl.multiple_of(t * n_keys, n_keys), n_keys), :] = word
            return carry

        lax.fori_loop(0, half, build, 0, unroll=8)

    slot = e_blk % 2
    acc_scr[...] += jnp.dot(p_scr[1 - slot], v_ref[...], preferred_element_type=F32)

    z = lax.dot_general(x_scr[...], u_ref[...], NT_DIMS, preferred_element_type=F32)
    first_row = jnp.minimum(e_blk, n_e - 1) * rows_per_step
    for c in range(rows_per_step):
        word = w_scr[pl.ds(first_row + c, half, stride=n_keys), :]
        w_a = lax.bitcast_convert_type(word & u32(0xFFFF0000), F32)
        w_b = lax.bitcast_convert_type(word << 16, F32)
        cols = slice(c * n_keys, (c + 1) * n_keys)
        p_scr[slot, 0:half, cols] = (w_a * _gelu_tanh(z[0:half, cols])).astype(BF16)
        p_scr[slot, half:tm, cols] = (w_b * _gelu_tanh(z[half:tm, cols])).astype(BF16)

    @pl.when(e_blk == n_e)
    def _():
        y_ref[...] = _layer_norm(DEEPNORM_ALPHA * h_ref[...] + acc_scr[...], g2_ref[...], b2_ref[...])


def _peer_dense(a, b, g, h, u_bf16, v_bf16, ln_g, ln_b, n_keys, tm, rows_per_step):
    n, dm = h.shape
    ne = a.shape[1]
    eb = rows_per_step * n_keys
    n_e = n_keys // rows_per_step
    tok = lambda w: pl.BlockSpec((tm, w), lambda i, e: (i, 0))
    full = lambda arr: pl.BlockSpec(arr.shape, lambda i, e: (0, 0))
    return pl.pallas_call(
        functools.partial(_peer_dense_kernel, tm=tm, n_keys=n_keys, rows_per_step=rows_per_step),
        grid=(n // tm, n_e + 1),
        in_specs=[tok(ne), tok(ne), tok(ne), tok(dm),
                  pl.BlockSpec((eb, dm), lambda i, e: (jnp.minimum(e, n_e - 1), 0)),
                  pl.BlockSpec((eb, dm), lambda i, e: (jnp.maximum(e - 1, 0), 0)),
                  full(ln_g), full(ln_b)],
        out_specs=tok(dm),
        out_shape=jax.ShapeDtypeStruct((n, dm), F32),
        scratch_shapes=[pltpu.VMEM((tm // 2 * n_keys, n_keys), jnp.uint32), pltpu.VMEM((tm, dm), BF16),
                        pltpu.VMEM((tm, dm), F32), pltpu.VMEM((2, tm, eb), BF16)],
        compiler_params=_cparams(("parallel", "arbitrary")),
        name="peer_experts",
    )(a, b, g, h, u_bf16, v_bf16, ln_g, ln_b)


def _layer(x, caches, h0, wts, ssm_mats):
    (w_in, ssm_d, w_glu, w_bs, w_ba, w_out, ln1_g, ln1_b, wq, k1, k2, pu, pv, ln2_g, ln2_b, slopes) = wts
    m, em, fm, am = ssm_mats
    bsz, seq, dm = x.shape
    n = bsz * seq
    ssm_width = ssm_d.shape[1]
    in_width = w_in.shape[1]
    qkv_width = (in_width - ssm_width - 2 * dm) // 3
    kv_half = HEADS_PER_GROUP * HEAD_DIM
    k_off, v_off = ssm_width + qkv_width, ssm_width + 2 * qkv_width
    x2d = x.reshape(n, dm)

    gate_off = ssm_width + 3 * qkv_width
    z = _proj(x2d, w_in[:, :gate_off], _pick_tile(n, 1024), _pick_tile(gate_off, 512))
    gates = _proj(x2d, w_in[:, gate_off:], _pick_tile(n, 1024), _pick_tile(2 * dm, 1024))
    z3 = z.reshape(bsz, seq, gate_off)

    new_kv = []
    if caches is None:
        qb = seq // max(DILATIONS)
        attn = _attn_prompt(z3, slopes, ssm_width, qkv_width, qb).reshape(n, kv_half)
        for gi, d in enumerate(DILATIONS):
            keep = min(qb * d, seq)
            kg = z3[:, seq - keep:, k_off + gi * kv_half:k_off + (gi + 1) * kv_half]
            vg = z3[:, seq - keep:, v_off + gi * kv_half:v_off + (gi + 1) * kv_half]
            new_kv.append(jnp.stack([kg, vg], axis=2).reshape(bsz, keep, 2, HEADS_PER_GROUP, HEAD_DIM))
    else:
        q = z3[:, :, ssm_width:k_off]
        k = z3[:, :, k_off:v_off]
        v = z3[:, :, v_off:v_off + qkv_width]
        attn = _attn_sample(slopes, q, k, v, list(caches)).reshape(n, kv_half)
        for gi in range(len(DILATIONS)):
            kg = k[:, :, gi * kv_half:(gi + 1) * kv_half]
            vg = v[:, :, gi * kv_half:(gi + 1) * kv_half]
            new_kv.append(jnp.stack([kg, vg], axis=2).reshape(bsz, seq, 2, HEADS_PER_GROUP, HEAD_DIM))

    ns = m.shape[0]
    p_state = em.shape[2] // (2 * SSM_PACK)
    lanes = ssm_width // ns
    u3 = z3[:, :, :ssm_width]
    exact_state = h0 is not None
    u2 = _to_chunk_rows(u3 if exact_state else u3.astype(BF16), SSM_CHUNK, lanes)
    h0s = (_state_to_slabs(h0, ns) if exact_state
           else jnp.zeros((ns, bsz, 2 * SSM_PACK * p_state), F32))
    y2, hfin = _ssm(u2, m, em, fm, am, h0s, bsz, exact_state)
    yc = _from_chunk_rows(y2, bsz, SSM_CHUNK, lanes)
    ssm_state = _state_from_slabs(hfin, p_state)

    h = _merge(yc, z, gates, attn, x2d, ssm_d, w_glu, w_bs, w_ba, w_out, ln1_g, ln1_b, _pick_tile(n, 256))
    a, b, g = _peer_topk(h, wq, k1, k2, _pick_tile(n, 1024))
    y = _peer_dense(a, b, g, h, pu, pv, ln2_g, ln2_b, k1.shape[0], _pick_tile(n, 512), 4)
    return y.reshape(bsz, seq, dm), new_kv, ssm_state


def kernel(x_prompt, x_sample, cache_kv_w128, cache_kv_w512, cache_kv_w2048, state_ssm, w_in, ssm_log_dt, ssm_a_re, ssm_a_im, ssm_b_re, ssm_b_im, ssm_c_re, ssm_c_im, ssm_d, ssm_w_glu, w_branch_ssm, w_branch_attn, w_out, ln1_g, ln1_b, peer_w_q, peer_sub_keys_1, peer_sub_keys_2, peer_u, peer_v, ln2_g, ln2_b):
    caches = (cache_kv_w128, cache_kv_w512, cache_kv_w2048)
    seq = x_prompt.shape[1]
    assert x_sample.shape[1] == SSM_CHUNK and seq % (SSM_CHUNK * max(DILATIONS)) == 0
    assert all(c.shape[1] == (seq // max(DILATIONS)) * d and c.shape[1] <= PAST_LEN
               for c, d in zip(caches, DILATIONS))
    n_heads = len(DILATIONS) * HEADS_PER_GROUP
    slopes = 2.0 ** (-8.0 * jnp.arange(1, n_heads + 1, dtype=F32) / n_heads)
    row = lambda a: a.reshape(1, -1).astype(F32)
    wts = (w_in.astype(BF16), row(ssm_d), ssm_w_glu.astype(BF16), w_branch_ssm.astype(BF16),
           w_branch_attn.astype(BF16), w_out.astype(BF16), row(ln1_g), row(ln1_b), peer_w_q.astype(BF16),
           peer_sub_keys_1.astype(F32), peer_sub_keys_2.astype(F32), peer_u.astype(BF16), peer_v.astype(BF16),
           row(ln2_g), row(ln2_b), slopes)
    ssm_mats = _ssm_layout(*_ssm_prep(ssm_log_dt, ssm_a_re, ssm_a_im, ssm_b_re, ssm_b_im, ssm_c_re, ssm_c_im,
                                      SSM_CHUNK), SSM_CHUNK)
    y_p, kv_p, ssm_p = _layer(x_prompt, None, None, wts, ssm_mats)
    y_s, kv_s, ssm_s = _layer(x_sample, caches, state_ssm, wts, ssm_mats)
    return (y_p, y_s, kv_p[0], kv_p[1], kv_p[2], ssm_p, kv_s[0], kv_s[1], kv_s[2], ssm_s)
```

```python
import functools
import math

import jax
import jax.numpy as jnp
from jax import lax
from jax.experimental import pallas as pl
from jax.experimental.pallas import tpu as pltpu

F32 = jnp.float32
BF16 = jnp.bfloat16
I32 = jnp.int32

HEAD_DIM = 128
HEADS_PER_GROUP = 4
DILATIONS = (1, 4, 16)
PAST_LEN = 16384
PEER_HEADS = 8
PEER_TOPK = 16
DEPTH = 1
DEEPNORM_ALPHA = (2.0 * DEPTH) ** 0.25
LN_EPS = 1e-5
SSM_CHUNK = 8
SSM_PACK = 8
NEG = -1e30
VMEM_LIMIT = 56 * 1024 * 1024

NT_DIMS = (((1,), (1,)), ((), ()))


def _cparams(sem):
    return pltpu.CompilerParams(dimension_semantics=sem, vmem_limit_bytes=VMEM_LIMIT)


def _gelu_tanh(x):
    return 0.5 * x * (1.0 + jnp.tanh(math.sqrt(2.0 / math.pi) * (x + 0.044715 * (x * x * x))))


def _sigmoid(x):
    return 1.0 / (1.0 + jnp.exp(-x))


def _layer_norm(x, g, b):
    mu = jnp.mean(x, axis=-1, keepdims=True)
    xc = x - mu
    var = jnp.mean(xc * xc, axis=-1, keepdims=True)
    return xc * lax.rsqrt(var + LN_EPS) * g + b


def _pick_tile(n, pref):
    t = min(n, pref)
    while n % t:
        t //= 2
    return t


def _proj_kernel(x_ref, w_ref, o_ref, xb_scr):
    @pl.when(pl.program_id(1) == 0)
    def _():
        xb_scr[...] = x_ref[...].astype(BF16)

    o_ref[...] = jnp.dot(xb_scr[...], w_ref[...], preferred_element_type=F32)


def _proj(x2d, w_bf16, tm, tn):
    n, d = x2d.shape
    nw = w_bf16.shape[1]
    return pl.pallas_call(
        _proj_kernel,
        grid=(n // tm, nw // tn),
        in_specs=[pl.BlockSpec((tm, d), lambda i, j: (i, 0)),
                  pl.BlockSpec((d, tn), lambda i, j: (0, j))],
        out_specs=pl.BlockSpec((tm, tn), lambda i, j: (i, j)),
        out_shape=jax.ShapeDtypeStruct((n, nw), F32),
        scratch_shapes=[pltpu.VMEM((tm, d), BF16)],
        compiler_params=_cparams(("parallel", "arbitrary")),
        name="in_proj",
    )(x2d, w_bf16)


def _rows(start, size, stride):
    return pl.ds(start, size) if stride == 1 else pl.ds(start, size, stride=stride)


def _attn_prompt_kernel(slopes_ref, q_ref, k_ref, v_ref, o_ref, m_ref, l_ref, *, seq, qb, scale):
    hs = pl.program_id(1)
    g = pl.program_id(2)
    n_groups = len(DILATIONS)
    for gi, d in enumerate(DILATIONS):
        @pl.when(g == gi)
        def _(gi=gi, d=d):
            slope = slopes_ref[gi * HEADS_PER_GROUP + hs] * float(d)
            for r in range(d):
                for i in range(seq // (d * qb)):
                    rq = _rows(r + i * qb * d, qb, d)
                    q = q_ref[0, rq, :].astype(BF16)
                    if i == 0:
                        rk, nk, koff = rq, qb, 0
                    else:
                        rk, nk, koff = _rows(r + (i - 1) * qb * d, 2 * qb, d), 2 * qb, qb
                    kk = k_ref[0, rk, :].astype(BF16)
                    vv = v_ref[0, rk, :].astype(BF16)
                    s = lax.dot_general(q, kk, NT_DIMS, preferred_element_type=F32) * scale
                    row = lax.broadcasted_iota(I32, (qb, nk), 0)
                    col = lax.broadcasted_iota(I32, (qb, nk), 1)
                    j = row + koff - col
                    s = s - slope * j.astype(F32)
                    s = jnp.where((j >= 0) & (j <= qb), s, NEG)
                    m_blk = jnp.max(s, axis=1, keepdims=True)
                    p = jnp.exp(s - m_blk)
                    l_blk = jnp.sum(p, axis=1, keepdims=True)
                    o_blk = jnp.dot(p.astype(BF16), vv, preferred_element_type=F32)
                    m_b = jnp.broadcast_to(m_blk, (qb, HEAD_DIM))
                    l_b = jnp.broadcast_to(l_blk, (qb, HEAD_DIM))
                    if gi == 0:
                        m_new, l_new, acc = m_b, l_b, o_blk
                    else:
                        m_old = m_ref[rq, :]
                        m_new = jnp.maximum(m_old, m_b)
                        a_old = jnp.exp(m_old - m_new)
                        a_blk = jnp.exp(m_b - m_new)
                        l_new = a_old * l_ref[rq, :] + a_blk * l_b
                        acc = a_old * o_ref[0, rq, :] + a_blk * o_blk
                    if gi == n_groups - 1:
                        o_ref[0, rq, :] = acc / l_new
                    else:
                        m_ref[rq, :] = m_new
                        l_ref[rq, :] = l_new
                        o_ref[0, rq, :] = acc


def _attn_prompt(z3, slopes, ssm_width, qkv_width, qb):
    bsz, seq, _ = z3.shape
    blk = lambda base: (lambda b, hs, g: (b, 0, base + g * HEADS_PER_GROUP + hs))
    qc, kc, vc = (ssm_width // HEAD_DIM, (ssm_width + qkv_width) // HEAD_DIM,
                  (ssm_width + 2 * qkv_width) // HEAD_DIM)
    return pl.pallas_call(
        functools.partial(_attn_prompt_kernel, seq=seq, qb=qb, scale=HEAD_DIM ** -0.5),
        grid=(bsz, HEADS_PER_GROUP, len(DILATIONS)),
        in_specs=[pl.BlockSpec(memory_space=pltpu.SMEM),
                  pl.BlockSpec((1, seq, HEAD_DIM), blk(qc)),
                  pl.BlockSpec((1, seq, HEAD_DIM), blk(kc)),
                  pl.BlockSpec((1, seq, HEAD_DIM), blk(vc))],
        out_specs=pl.BlockSpec((1, seq, HEAD_DIM), lambda b, hs, g: (b, 0, hs)),
        out_shape=jax.ShapeDtypeStruct((bsz, seq, HEADS_PER_GROUP * HEAD_DIM), F32),
        scratch_shapes=[pltpu.VMEM((seq, HEAD_DIM), F32), pltpu.VMEM((seq, HEAD_DIM), F32)],
        compiler_params=_cparams(("parallel", "parallel", "arbitrary")),
        name="attn_prompt",
    )(slopes, z3, z3, z3)


def _attn_sample_kernel(slopes_ref, q_ref, k_ref, v_ref, c0_ref, c1_ref, c2_ref, o_ref, *, n_new, scale):
    caches = (c0_ref, c1_ref, c2_ref)
    for hs in range(HEADS_PER_GROUP):
        pieces = []
        for gi, d in enumerate(DILATIONS):
            c_ref = caches[gi]
            win = c_ref.shape[1]
            hcol = (gi * HEADS_PER_GROUP + hs) * HEAD_DIM
            slope = slopes_ref[gi * HEADS_PER_GROUP + hs]
            q = q_ref[0, :, hcol:hcol + HEAD_DIM].astype(BF16)
            k_new = k_ref[0, :, hcol:hcol + HEAD_DIM].astype(BF16)
            v_new = v_ref[0, :, hcol:hcol + HEAD_DIM].astype(BF16)
            k_old = c_ref[0, :, 0, hs, :].astype(BF16)
            v_old = c_ref[0, :, 1, hs, :].astype(BF16)
            for kk, vv, nk, base in ((k_old, v_old, win, win), (k_new, v_new, n_new, 0)):
                s = lax.dot_general(q, kk, NT_DIMS, preferred_element_type=F32) * scale
                row = lax.broadcasted_iota(I32, (n_new, nk), 0)
                col = lax.broadcasted_iota(I32, (n_new, nk), 1)
                offs = base + row - col
                valid = (offs >= 0) & (offs <= win) & ((offs & (d - 1)) == 0)
                s = jnp.where(valid, s - slope * offs.astype(F32), NEG)
                pieces.append((s, vv))
        m = pieces[0][0].max(axis=1, keepdims=True)
        for s, _ in pieces[1:]:
            m = jnp.maximum(m, s.max(axis=1, keepdims=True))
        l = jnp.zeros((n_new, 1), F32)
        acc = jnp.zeros((n_new, HEAD_DIM), F32)
        for s, vv in pieces:
            p = jnp.exp(s - m)
            l = l + p.sum(axis=1, keepdims=True)
            acc = acc + jnp.dot(p.astype(BF16), vv, preferred_element_type=F32)
        o_ref[0, :, hs * HEAD_DIM:(hs + 1) * HEAD_DIM] = acc / l


def _attn_sample(slopes, q, k, v, caches):
    bsz, n_new, qkv_width = q.shape
    assert all(d & (d - 1) == 0 for d in DILATIONS)
    row_spec = lambda a: pl.BlockSpec((1,) + a.shape[1:], lambda b: (b,) + (0,) * (a.ndim - 1))
    return pl.pallas_call(
        functools.partial(_attn_sample_kernel, n_new=n_new, scale=HEAD_DIM ** -0.5),
        grid=(bsz,),
        in_specs=[pl.BlockSpec(memory_space=pltpu.SMEM), row_spec(q), row_spec(k), row_spec(v)]
                 + [row_spec(c) for c in caches],
        out_specs=pl.BlockSpec((1, n_new, HEADS_PER_GROUP * HEAD_DIM), lambda b: (b, 0, 0)),
        out_shape=jax.ShapeDtypeStruct((bsz, n_new, HEADS_PER_GROUP * HEAD_DIM), F32),
        compiler_params=_cparams(("parallel",)),
        name="attn_sample",
    )(slopes, q, k, v, *caches)


def _ssm_prep_kernel(ldt_ref, ar_r_ref, ai_r_ref, ar_c_ref, ai_c_ref, br_ref, bi_ref, cr_ref, ci_ref,
                     kt_ref, e_ref, f_ref, at_ref, *, chunk):
    dt = jnp.exp(ldt_ref[0])

    def discretise(ar, ai):
        mag = jnp.exp(ar * dt)
        return mag * jnp.cos(ai * dt), mag * jnp.sin(ai * dt)

    ar_r, ai_r = ar_r_ref[0], ai_r_ref[0]
    ar_c, ai_c = ar_c_ref[0], ai_c_ref[0]
    abr_r, abi_r = discretise(ar_r, ai_r)
    abr_c, abi_c = discretise(ar_c, ai_c)
    den = ar_c * ar_c + ai_c * ai_c
    f_re = ((abr_c - 1.0) * ar_c + abi_c * ai_c) / den
    f_im = (abi_c * ar_c - (abr_c - 1.0) * ai_c) / den
    b_re, b_im = br_ref[0], bi_ref[0]
    bb_re = f_re * b_re - f_im * b_im
    bb_im = f_re * b_im + f_im * b_re
    c_re, c_im = cr_ref[0], ci_ref[0]

    def powers(re1, im1):
        pr, pi = [jnp.ones_like(re1)], [jnp.zeros_like(re1)]
        for _ in range(chunk):
            pr.append(pr[-1] * re1 - pi[-1] * im1)
            pi.append(pr[-2] * im1 + pi[-1] * re1)
        return pr, pi

    pr_r, pi_r = powers(abr_r, abi_r)
    pr_c, pi_c = powers(abr_c, abi_c)
    hi = lax.Precision.HIGHEST
    for t in range(chunk):
        cp_re = c_re * pr_r[t] - c_im * pi_r[t]
        cp_im = c_re * pi_r[t] + c_im * pr_r[t]
        kt_ref[0, t] = (jnp.dot(cp_re, bb_re, precision=hi, preferred_element_type=F32)
                        - jnp.dot(cp_im, bb_im, precision=hi, preferred_element_type=F32))
        qr, qi = pr_c[chunk - 1 - t], pi_c[chunk - 1 - t]
        e_ref[0, t, 0] = qr * bb_re - qi * bb_im
        e_ref[0, t, 1] = qr * bb_im + qi * bb_re
        f_ref[0, t, 0] = c_re * pr_r[t + 1] - c_im * pi_r[t + 1]
        f_ref[0, t, 1] = -(c_re * pi_r[t + 1] + c_im * pr_r[t + 1])
    at_ref[0, 0:1, :] = pr_r[chunk]
    at_ref[0, 1:2, :] = pi_r[chunk]


def _ssm_prep(log_dt, a_re, a_im, b_re, b_im, c_re, c_im, chunk):
    g, p = a_re.shape
    h = b_re.shape[2]
    g3 = lambda *s: pl.BlockSpec((1,) + s, lambda i: (i,) + (0,) * len(s))
    return pl.pallas_call(
        functools.partial(_ssm_prep_kernel, chunk=chunk),
        grid=(g,),
        in_specs=[g3(1, 1), g3(1, p), g3(1, p), g3(p, 1), g3(p, 1), g3(p, h), g3(p, h), g3(h, p), g3(h, p)],
        out_specs=[g3(chunk, h, h), g3(chunk, 2, p, h), g3(chunk, 2, h, p), g3(2, p)],
        out_shape=[jax.ShapeDtypeStruct((g, chunk, h, h), F32),
                   jax.ShapeDtypeStruct((g, chunk, 2, p, h), F32),
                   jax.ShapeDtypeStruct((g, chunk, 2, h, p), F32),
                   jax.ShapeDtypeStruct((g, 2, p), F32)],
        compiler_params=_cparams(("parallel",)),
        name="ssm_prep",
    )(log_dt.reshape(g, 1, 1), a_re.reshape(g, 1, p), a_im.reshape(g, 1, p),
      a_re.reshape(g, p, 1), a_im.reshape(g, p, 1), b_re, b_im, c_re, c_im)


def _ssm_layout(kt, e, f, at, chunk):
    g, _, h, _ = kt.shape
    p = at.shape[2]
    ns = g // SSM_PACK
    eye = jnp.eye(SSM_PACK, dtype=F32)
    lag = jnp.arange(chunk)[None, :] - jnp.arange(chunk)[:, None]
    ktoe = jnp.where((lag >= 0)[None, :, :, None, None],
                     kt[:, jnp.clip(lag, 0, chunk - 1)], 0.0)
    ktoe = ktoe.reshape(ns, SSM_PACK, chunk, chunk, h, h)
    m = jnp.einsum('ogijkh,gq->oighjqk', ktoe, eye).reshape(ns, chunk * SSM_PACK * h, chunk * SSM_PACK * h)
    e6 = e.reshape(ns, SSM_PACK, chunk, 2, p, h)
    em = jnp.einsum('ogirph,gq->oighrqp', e6, eye).reshape(ns, chunk * SSM_PACK * h, 2 * SSM_PACK * p)
    f6 = f.reshape(ns, SSM_PACK, chunk, 2, h, p)
    fm = jnp.einsum('ogirhp,gq->orgpiqh', f6, eye).reshape(ns, 2 * SSM_PACK * p, chunk * SSM_PACK * h)
    am = at.reshape(ns, SSM_PACK, 2, p).transpose(0, 2, 1, 3).reshape(ns, 2, SSM_PACK * p)
    return m.astype(BF16), em, fm.astype(BF16), am


def _ssm_kernel(u_ref, m_ref, e_ref, f_ref, a_ref, h0_ref, y_ref, hfin_ref, h_scr, s_scr, hp_scr,
                *, bt, n_chunks, exact_state):
    half = a_ref.shape[2]

    @pl.when(pl.program_id(1) == 0)
    def _():
        h_scr[...] = h0_ref[0]

    u = u_ref[...]
    if exact_state:
        s_scr[...] = jnp.dot(u, e_ref[0], precision=lax.Precision.HIGHEST, preferred_element_type=F32)
    else:
        s_scr[...] = jnp.dot(u.astype(BF16), e_ref[0].astype(BF16), preferred_element_type=F32)
    a_re = jnp.broadcast_to(a_ref[0, 0:1, :], (bt, half))
    a_im = jnp.broadcast_to(a_ref[0, 1:2, :], (bt, half))

    def step(c, carry):
        h_re, h_im = carry
        rows = pl.ds(pl.multiple_of(c * bt, bt), bt)
        hp_scr[rows, 0:half] = h_re
        hp_scr[rows, half:2 * half] = h_im
        n_re = a_re * h_re - a_im * h_im + s_scr[rows, 0:half]
        n_im = a_re * h_im + a_im * h_re + s_scr[rows, half:2 * half]
        return n_re, n_im

    h_re, h_im = lax.fori_loop(0, n_chunks, step, (h_scr[:, 0:half], h_scr[:, half:2 * half]))
    h_scr[:, 0:half] = h_re
    h_scr[:, half:2 * half] = h_im
    hfin_ref[0] = h_scr[...]
    y_ref[...] = (jnp.dot(u.astype(BF16), m_ref[0], preferred_element_type=F32)
                  + jnp.dot(hp_scr[...].astype(BF16), f_ref[0], preferred_element_type=F32)).astype(y_ref.dtype)


def _ssm(u2, m, em, fm, am, h0, bt, exact_state):
    rows, _ = u2.shape
    ns, sw, _ = m.shape
    st = em.shape[2]
    tr = _pick_tile(rows // bt, 64) * bt
    slab = lambda i, r: (i, 0, 0)
    return pl.pallas_call(
        functools.partial(_ssm_kernel, bt=bt, n_chunks=tr // bt, exact_state=exact_state),
        grid=(ns, rows // tr),
        in_specs=[pl.BlockSpec((tr, sw), lambda i, r: (r, i)),
                  pl.BlockSpec((1, sw, sw), slab), pl.BlockSpec((1, sw, st), slab),
                  pl.BlockSpec((1, st, sw), slab), pl.BlockSpec((1, 2, st // 2), slab),
                  pl.BlockSpec((1, bt, st), slab)],
        out_specs=[pl.BlockSpec((tr, sw), lambda i, r: (r, i)), pl.BlockSpec((1, bt, st), slab)],
        out_shape=[jax.ShapeDtypeStruct((rows, ns * sw), BF16), jax.ShapeDtypeStruct((ns, bt, st), F32)],
        scratch_shapes=[pltpu.VMEM((bt, st), F32), pltpu.VMEM((tr, st), F32), pltpu.VMEM((tr, st), F32)],
        compiler_params=_cparams(("parallel", "arbitrary")),
        name="ssm_scan",
    )(u2, m, em, fm, am, h0)


def _to_chunk_rows(u, chunk, lanes):
    b, l, w = u.shape
    c = l // chunk
    return u.reshape(b, c, chunk, w // lanes, lanes).transpose(1, 0, 3, 2, 4).reshape(c * b, chunk * w)


def _from_chunk_rows(y2, b, chunk, lanes):
    rows, cw = y2.shape
    c, w = rows // b, cw // chunk
    return y2.reshape(c, b, w // lanes, chunk, lanes).transpose(1, 0, 3, 2, 4).reshape(b * c * chunk, w)


def _state_to_slabs(h, ns):
    b, g, p, _ = h.shape
    return h.reshape(b, ns, SSM_PACK, p, 2).transpose(1, 0, 4, 2, 3).reshape(ns, b, 2 * SSM_PACK * p)


def _state_from_slabs(hs, p):
    ns, b, _ = hs.shape
    return hs.reshape(ns, b, 2, SSM_PACK, p).transpose(1, 0, 3, 4, 2).reshape(b, ns * SSM_PACK, p, 2)


def _merge_kernel(yc_ref, u_ref, gs_ref, ga_ref, at_ref, x_ref, d_ref, wglu_ref, wbs_ref, wba_ref, wout_ref,
                  g1_ref, b1_ref, h_ref):
    y = yc_ref[...].astype(F32) + d_ref[...] * u_ref[...]
    gl = _gelu_tanh(y)
    so = gl * _sigmoid(jnp.dot(gl.astype(BF16), wglu_ref[...], preferred_element_type=F32))
    merged = (_sigmoid(gs_ref[...]) * jnp.dot(so.astype(BF16), wbs_ref[...], preferred_element_type=F32)
              + _sigmoid(ga_ref[...]) * jnp.dot(at_ref[...].astype(BF16), wba_ref[...], preferred_element_type=F32))
    mix = jnp.dot(merged.astype(BF16), wout_ref[...], preferred_element_type=F32)
    h_ref[...] = _layer_norm(DEEPNORM_ALPHA * x_ref[...] + mix, g1_ref[...], b1_ref[...])


def _merge(yc, z, gates, attn, x2d, d_skip, w_glu, w_bs, w_ba, w_out, ln_g, ln_b, tm):
    n, dm = x2d.shape
    aw = attn.shape[1]
    ssm_width = yc.shape[1]
    row = lambda w, cb=0: pl.BlockSpec((tm, w), lambda i: (i, cb))
    full = lambda a: pl.BlockSpec(a.shape, lambda i: (0, 0), pipeline_mode=pl.Buffered(1))
    return pl.pallas_call(
        _merge_kernel,
        grid=(n // tm,),
        in_specs=[row(ssm_width), row(ssm_width), row(dm, 0), row(dm, 1), row(aw), row(dm),
                  full(d_skip), full(w_glu), full(w_bs), full(w_ba), full(w_out), full(ln_g), full(ln_b)],
        out_specs=row(dm),
        out_shape=jax.ShapeDtypeStruct((n, dm), F32),
        compiler_params=_cparams(("parallel",)),
        name="merge",
    )(yc, z, gates, gates, attn, x2d, d_skip, w_glu, w_bs, w_ba, w_out, ln_g, ln_b)


def _order(x, y):
    if y is None:
        return x, None
    if x is None:
        return y, None
    (va, ta), (vb, tb) = x, y
    first = (va > vb) | ((va == vb) & (ta < tb))
    return ((jnp.where(first, va, vb), jnp.where(first, ta, tb)),
            (jnp.where(first, vb, va), jnp.where(first, tb, ta)))


def _bitonic_merge(xs):
    n = len(xs)
    if n == 1:
        return xs
    pairs = [_order(xs[i], xs[i + n // 2]) for i in range(n // 2)]
    return _bitonic_merge([p[0] for p in pairs]) + _bitonic_merge([p[1] for p in pairs])


def _sort_desc(xs):
    n = len(xs)
    if n == 1:
        return xs
    return _bitonic_merge(_sort_desc(xs[:n // 2]) + _sort_desc(xs[n // 2:])[::-1])


def _top_of_two(xs, ys):
    n = len(xs)
    return _bitonic_merge([_order(xs[i], ys[n - 1 - i])[0] for i in range(n)])


def _top_sorted(items, k):
    items = items + [None] * (-len(items) % k)
    group = lambda s: _sort_desc([it() if callable(it) else it for it in items[s:s + k]])
    run = group(0)
    for s in range(k, len(items), k):
        run = _top_of_two(run, group(s))
    return run


def _peer_topk_kernel(h_ref, wq_ref, k1_ref, k2_ref, a_ref, b_ref, g_ref, s_scr, o_scr, *, half, n_keys, tm):
    k = PEER_TOPK
    nj = tm // 128
    key_bits = n_keys.bit_length() - 1
    hi = lax.Precision.HIGHEST
    x = h_ref[...].astype(BF16)

    def half_top(keys_ref, qh):
        s = lax.dot_general(keys_ref[...], qh, NT_DIMS, precision=hi, preferred_element_type=F32)
        for kh in range(n_keys // 8):
            for j in range(nj):
                s_scr[pl.ds((kh * nj + j) * 8, 8), :] = s[kh * 8:(kh + 1) * 8, j * 128:(j + 1) * 128]
        def item(key):
            start = (key // 8) * nj * 8 + key % 8
            rows = pl.ds(start, nj, stride=8) if nj > 1 else pl.ds(start, 1)
            return lambda: (s_scr[rows, :], jnp.full((nj, 128), key, I32))

        return _top_sorted([item(key) for key in range(n_keys)], k)

    for hd in range(PEER_HEADS):
        q = jnp.dot(x, wq_ref[:, hd * 2 * half:(hd + 1) * 2 * half], preferred_element_type=F32)
        t1 = half_top(k1_ref, q[:, :half])
        t2 = half_top(k2_ref, q[:, half:])
        cands = []
        for r in range(k):
            for c in range(k):
                if (r + 1) * (c + 1) <= k:
                    tag = (t1[r][1] << key_bits) + t2[c][1] + ((r * k + c) << (2 * key_bits))
                    cands.append((t1[r][0] + t2[c][0], tag))
        top = _top_sorted(cands, k)
        ex = [jnp.exp(v - top[0][0]) for v, _ in top]
        tot = ex[0]
        for e in ex[1:]:
            tot = tot + e
        for i, (_, tag) in enumerate(top):
            row = pl.ds((hd * k + i) * nj, nj)
            o_scr[0, row, :] = ((tag >> key_bits) & (n_keys - 1)).astype(F32)
            o_scr[1, row, :] = (tag & (n_keys - 1)).astype(F32)
            o_scr[2, row, :] = ex[i] / tot
    ne = PEER_HEADS * k
    for j in range(nj):
        rows = pl.ds(j, ne, stride=nj) if nj > 1 else pl.ds(0, ne)
        tok = pl.ds(j * 128, 128)
        a_ref[tok, :] = o_scr[0, rows, :].T.astype(I32)
        b_ref[tok, :] = o_scr[1, rows, :].T.astype(I32)
        g_ref[tok, :] = o_scr[2, rows, :].T


def _peer_topk(h, wq_bf16, k1, k2, tm):
    n, dm = h.shape
    n_keys, half = k1.shape
    ne = PEER_HEADS * PEER_TOPK
    assert n_keys & (n_keys - 1) == 0 and tm % 128 == 0 and ne == 128
    assert (PEER_TOPK * PEER_TOPK) << (2 * (n_keys.bit_length() - 1)) < 2 ** 31
    full = lambda a: pl.BlockSpec(a.shape, lambda i: (0, 0), pipeline_mode=pl.Buffered(1))
    out = lambda: pl.BlockSpec((tm, ne), lambda i: (i, 0))
    return pl.pallas_call(
        functools.partial(_peer_topk_kernel, half=half, n_keys=n_keys, tm=tm),
        grid=(n // tm,),
        in_specs=[pl.BlockSpec((tm, dm), lambda i: (i, 0)), full(wq_bf16), full(k1), full(k2)],
        out_specs=[out(), out(), out()],
        out_shape=[jax.ShapeDtypeStruct((n, ne), I32), jax.ShapeDtypeStruct((n, ne), I32),
                   jax.ShapeDtypeStruct((n, ne), F32)],
        scratch_shapes=[pltpu.VMEM((n_keys * tm // 128, 128), F32), pltpu.VMEM((3, ne * tm // 128, 128), F32)],
        compiler_params=_cparams(("parallel",)),
        name="peer_topk",
    )(h, wq_bf16, k1, k2)


def _peer_dense_kernel(a_ref, b_ref, g_ref, h_ref, u_ref, v_ref, g2_ref, b2_ref, y_ref, w_scr, x_scr, acc_scr, p_scr,
                       *, tm, n_keys, rows_per_step):
    e_blk = pl.program_id(1)
    n_e = pl.num_programs(1) - 1
    half = tm // 2
    ne = a_ref.shape[1]
    u32 = jnp.uint32

    @pl.when(e_blk == 0)
    def _():
        x_scr[...] = h_ref[...].astype(BF16)
        acc_scr[...] = jnp.zeros_like(acc_scr)
        p_scr[1] = jnp.zeros(p_scr.shape[1:], BF16)
        key_iota = lax.broadcasted_iota(I32, (n_keys, ne), 0)

        def gate_bits(t):
            a_row = a_ref[pl.ds(t, 1), :]
            b_row = b_ref[pl.ds(t, 1), :]
            g_row = g_ref[pl.ds(t, 1), :]
            sel_a = jnp.where(key_iota == a_row, 1.0, 0.0).astype(BF16)
            sel_b = jnp.where(key_iota == b_row, g_row, 0.0).astype(BF16)
            w = lax.dot_general(sel_a, sel_b, NT_DIMS, preferred_element_type=F32)
            return lax.bitcast_convert_type(w.astype(BF16).astype(F32), u32)

        def build(t, carry):
            word = gate_bits(t) | (gate_bits(t + half) >> 16)
            w_scr[pl.ds(pl.multiple_of(t * n_keys, n_keys), n_keys), :] = word
            return carry

        lax.fori_loop(0, half, build, 0, unroll=8)

    slot = e_blk % 2
    acc_scr[...] += jnp.dot(p_scr[1 - slot], v_ref[...], preferred_element_type=F32)

    z = lax.dot_general(x_scr[...], u_ref[...], NT_DIMS, preferred_element_type=F32)
    first_row = jnp.minimum(e_blk, n_e - 1) * rows_per_step
    for c in range(rows_per_step):
        word = w_scr[pl.ds(first_row + c, half, stride=n_keys), :]
        w_a = lax.bitcast_convert_type(word & u32(0xFFFF0000), F32)
        w_b = lax.bitcast_convert_type(word << 16, F32)
        cols = slice(c * n_keys, (c + 1) * n_keys)
        p_scr[slot, 0:half, cols] = (w_a * _gelu_tanh(z[0:half, cols])).astype(BF16)
        p_scr[slot, half:tm, cols] = (w_b * _gelu_tanh(z[half:tm, cols])).astype(BF16)

    @pl.when(e_blk == n_e)
    def _():
        y_ref[...] = _layer_norm(DEEPNORM_ALPHA * h_ref[...] + acc_scr[...], g2_ref[...], b2_ref[...])


def _peer_dense(a, b, g, h, u_bf16, v_bf16, ln_g, ln_b, n_keys, tm, rows_per_step):
    n, dm = h.shape
    ne = a.shape[1]
    eb = rows_per_step * n_keys
    n_e = n_keys // rows_per_step
    tok = lambda w: pl.BlockSpec((tm, w), lambda i, e: (i, 0))
    full = lambda arr: pl.BlockSpec(arr.shape, lambda i, e: (0, 0))
    return pl.pallas_call(
        functools.partial(_peer_dense_kernel, tm=tm, n_keys=n_keys, rows_per_step=rows_per_step),
        grid=(n // tm, n_e + 1),
        in_specs=[tok(ne), tok(ne), tok(ne), tok(dm),
                  pl.BlockSpec((eb, dm), lambda i, e: (jnp.minimum(e, n_e - 1), 0)),
                  pl.BlockSpec((eb, dm), lambda i, e: (jnp.maximum(e - 1, 0), 0)),
                  full(ln_g), full(ln_b)],
        out_specs=tok(dm),
        out_shape=jax.ShapeDtypeStruct((n, dm), F32),
        scratch_shapes=[pltpu.VMEM((tm // 2 * n_keys, n_keys), jnp.uint32), pltpu.VMEM((tm, dm), BF16),
                        pltpu.VMEM((tm, dm), F32), pltpu.VMEM((2, tm, eb), BF16)],
        compiler_params=_cparams(("parallel", "arbitrary")),
        name="peer_experts",
    )(a, b, g, h, u_bf16, v_bf16, ln_g, ln_b)


def _layer(x, caches, h0, wts, ssm_mats):
    (w_in, ssm_d, w_glu, w_bs, w_ba, w_out, ln1_g, ln1_b, wq, k1, k2, pu, pv, ln2_g, ln2_b, slopes) = wts
    m, em, fm, am = ssm_mats
    bsz, seq, dm = x.shape
    n = bsz * seq
    ssm_width = ssm_d.shape[1]
    in_width = w_in.shape[1]
    qkv_width = (in_width - ssm_width - 2 * dm) // 3
    kv_half = HEADS_PER_GROUP * HEAD_DIM
    k_off, v_off = ssm_width + qkv_width, ssm_width + 2 * qkv_width
    x2d = x.reshape(n, dm)

    gate_off = ssm_width + 3 * qkv_width
    z = _proj(x2d, w_in[:, :gate_off], _pick_tile(n, 1024), _pick_tile(gate_off, 512))
    gates = _proj(x2d, w_in[:, gate_off:], _pick_tile(n, 1024), _pick_tile(2 * dm, 1024))
    z3 = z.reshape(bsz, seq, gate_off)

    new_kv = []
    if caches is None:
        qb = seq // max(DILATIONS)
        attn = _attn_prompt(z3, slopes, ssm_width, qkv_width, qb).reshape(n, kv_half)
        for gi, d in enumerate(DILATIONS):
            keep = min(qb * d, seq)
            kg = z3[:, seq - keep:, k_off + gi * kv_half:k_off + (gi + 1) * kv_half]
            vg = z3[:, seq - keep:, v_off + gi * kv_half:v_off + (gi + 1) * kv_half]
            new_kv.append(jnp.stack([kg, vg], axis=2).reshape(bsz, keep, 2, HEADS_PER_GROUP, HEAD_DIM))
    else:
        q = z3[:, :, ssm_width:k_off]
        k = z3[:, :, k_off:v_off]
        v = z3[:, :, v_off:v_off + qkv_width]
        attn = _attn_sample(slopes, q, k, v, list(caches)).reshape(n, kv_half)
        for gi in range(len(DILATIONS)):
            kg = k[:, :, gi * kv_half:(gi + 1) * kv_half]
            vg = v[:, :, gi * kv_half:(gi + 1) * kv_half]
            new_kv.append(jnp.stack([kg, vg], axis=2).reshape(bsz, seq, 2, HEADS_PER_GROUP, HEAD_DIM))

    ns = m.shape[0]
    p_state = em.shape[2] // (2 * SSM_PACK)
    lanes = ssm_width // ns
    u3 = z3[:, :, :ssm_width]
    exact_state = h0 is not None
    u2 = _to_chunk_rows(u3 if exact_state else u3.astype(BF16), SSM_CHUNK, lanes)
    h0s = (_state_to_slabs(h0, ns) if exact_state
           else jnp.zeros((ns, bsz, 2 * SSM_PACK * p_state), F32))
    y2, hfin = _ssm(u2, m, em, fm, am, h0s, bsz, exact_state)
    yc = _from_chunk_rows(y2, bsz, SSM_CHUNK, lanes)
    ssm_state = _state_from_slabs(hfin, p_state)

    h = _merge(yc, z, gates, attn, x2d, ssm_d, w_glu, w_bs, w_ba, w_out, ln1_g, ln1_b, _pick_tile(n, 256))
    a, b, g = _peer_topk(h, wq, k1, k2, _pick_tile(n, 1024))
    y = _peer_dense(a, b, g, h, pu, pv, ln2_g, ln2_b, k1.shape[0], _pick_tile(n, 512), 4)
    return y.reshape(bsz, seq, dm), new_kv, ssm_state


def kernel(x_prompt, x_sample, cache_kv_w128, cache_kv_w512, cache_kv_w2048, state_ssm, w_in, ssm_log_dt, ssm_a_re, ssm_a_im, ssm_b_re, ssm_b_im, ssm_c_re, ssm_c_im, ssm_d, ssm_w_glu, w_branch_ssm, w_branch_attn, w_out, ln1_g, ln1_b, peer_w_q, peer_sub_keys_1, peer_sub_keys_2, peer_u, peer_v, ln2_g, ln2_b):
    caches = (cache_kv_w128, cache_kv_w512, cache_kv_w2048)
    seq = x_prompt.shape[1]
    assert x_sample.shape[1] == SSM_CHUNK and seq % (SSM_CHUNK * max(DILATIONS)) == 0
    assert all(c.shape[1] == (seq // max(DILATIONS)) * d and c.shape[1] <= PAST_LEN
               for c, d in zip(caches, DILATIONS))
    n_heads = len(DILATIONS) * HEADS_PER_GROUP
    slopes = 2.0 ** (-8.0 * jnp.arange(1, n_heads + 1, dtype=F32) / n_heads)
    row = lambda a: a.reshape(1, -1).astype(F32)
    wts = (w_in.astype(BF16), row(ssm_d), ssm_w_glu.astype(BF16), w_branch_ssm.astype(BF16),
           w_branch_attn.astype(BF16), w_out.astype(BF16), row(ln1_g), row(ln1_b), peer_w_q.astype(BF16),
           peer_sub_keys_1.astype(F32), peer_sub_keys_2.astype(F32), peer_u.astype(BF16), peer_v.astype(BF16),
           row(ln2_g), row(ln2_b), slopes)
    ssm_mats = _ssm_layout(*_ssm_prep(ssm_log_dt, ssm_a_re, ssm_a_im, ssm_b_re, ssm_b_im, ssm_c_re, ssm_c_im,
                                      SSM_CHUNK), SSM_CHUNK)
    y_p, kv_p, ssm_p = _layer(x_prompt, None, None, wts, ssm_mats)
    y_s, kv_s, ssm_s = _layer(x_sample, caches, state_ssm, wts, ssm_mats)
    return (y_p, y_s, kv_p[0], kv_p[1], kv_p[2], ssm_p, kv_s[0], kv_s[1], kv_s[2], ssm_s)
```

```python
import functools
import math

import jax
import jax.numpy as jnp
from jax import lax
from jax.experimental import pallas as pl
from jax.experimental.pallas import tpu as pltpu

F32 = jnp.float32
BF16 = jnp.bfloat16
I32 = jnp.int32

HEAD_DIM = 128
HEADS_PER_GROUP = 4
DILATIONS = (1, 4, 16)
PAST_LEN = 16384
PEER_HEADS = 8
PEER_TOPK = 16
DEPTH = 1
DEEPNORM_ALPHA = (2.0 * DEPTH) ** 0.25
LN_EPS = 1e-5
SSM_CHUNK = 8
SSM_PACK = 8
NEG = -1e30
VMEM_LIMIT = 56 * 1024 * 1024

NT_DIMS = (((1,), (1,)), ((), ()))


def _cparams(sem):
    return pltpu.CompilerParams(dimension_semantics=sem, vmem_limit_bytes=VMEM_LIMIT)


def _gelu_tanh(x):
    return 0.5 * x * (1.0 + jnp.tanh(math.sqrt(2.0 / math.pi) * (x + 0.044715 * (x * x * x))))


def _sigmoid(x):
    return 1.0 / (1.0 + jnp.exp(-x))


def _layer_norm(x, g, b):
    mu = jnp.mean(x, axis=-1, keepdims=True)
    xc = x - mu
    var = jnp.mean(xc * xc, axis=-1, keepdims=True)
    return xc * lax.rsqrt(var + LN_EPS) * g + b


def _pick_tile(n, pref):
    t = min(n, pref)
    while n % t:
        t //= 2
    return t


def _proj_kernel(x_ref, w_ref, o_ref):
    o_ref[...] = jnp.dot(x_ref[...].astype(BF16), w_ref[...], preferred_element_type=F32)


def _proj(x2d, w_bf16, tm, tn):
    n, d = x2d.shape
    nw = w_bf16.shape[1]
    return pl.pallas_call(
        _proj_kernel,
        grid=(n // tm, nw // tn),
        in_specs=[pl.BlockSpec((tm, d), lambda i, j: (i, 0)),
                  pl.BlockSpec((d, tn), lambda i, j: (0, j))],
        out_specs=pl.BlockSpec((tm, tn), lambda i, j: (i, j)),
        out_shape=jax.ShapeDtypeStruct((n, nw), F32),
        compiler_params=_cparams(("parallel", "arbitrary")),
        name="in_proj",
    )(x2d, w_bf16)


def _rows(start, size, stride):
    return pl.ds(start, size) if stride == 1 else pl.ds(start, size, stride=stride)


def _attn_prompt_kernel(slopes_ref, q_ref, k_ref, v_ref, o_ref, m_ref, l_ref, *, seq, qb, scale):
    hs = pl.program_id(1)
    g = pl.program_id(2)
    n_groups = len(DILATIONS)
    for gi, d in enumerate(DILATIONS):
        @pl.when(g == gi)
        def _(gi=gi, d=d):
            slope = slopes_ref[gi * HEADS_PER_GROUP + hs] * float(d)
            for r in range(d):
                for i in range(seq // (d * qb)):
                    rq = _rows(r + i * qb * d, qb, d)
                    q = q_ref[0, rq, :].astype(BF16)
                    if i == 0:
                        rk, nk, koff = rq, qb, 0
                    else:
                        rk, nk, koff = _rows(r + (i - 1) * qb * d, 2 * qb, d), 2 * qb, qb
                    kk = k_ref[0, rk, :].astype(BF16)
                    vv = v_ref[0, rk, :].astype(BF16)
                    s = lax.dot_general(q, kk, NT_DIMS, preferred_element_type=F32) * scale
                    row = lax.broadcasted_iota(I32, (qb, nk), 0)
                    col = lax.broadcasted_iota(I32, (qb, nk), 1)
                    j = row + koff - col
                    s = s - slope * j.astype(F32)
                    s = jnp.where((j >= 0) & (j <= qb), s, NEG)
                    m_blk = jnp.max(s, axis=1, keepdims=True)
                    p = jnp.exp(s - m_blk)
                    l_blk = jnp.sum(p, axis=1, keepdims=True)
                    o_blk = jnp.dot(p.astype(BF16), vv, preferred_element_type=F32)
                    m_b = jnp.broadcast_to(m_blk, (qb, HEAD_DIM))
                    l_b = jnp.broadcast_to(l_blk, (qb, HEAD_DIM))
                    if gi == 0:
                        m_new, l_new, acc = m_b, l_b, o_blk
                    else:
                        m_old = m_ref[rq, :]
                        m_new = jnp.maximum(m_old, m_b)
                        a_old = jnp.exp(m_old - m_new)
                        a_blk = jnp.exp(m_b - m_new)
                        l_new = a_old * l_ref[rq, :] + a_blk * l_b
                        acc = a_old * o_ref[0, rq, :] + a_blk * o_blk
                    if gi == n_groups - 1:
                        o_ref[0, rq, :] = acc / l_new
                    else:
                        m_ref[rq, :] = m_new
                        l_ref[rq, :] = l_new
                        o_ref[0, rq, :] = acc


def _attn_prompt(z3, slopes, ssm_width, qkv_width, qb):
    bsz, seq, _ = z3.shape
    blk = lambda base: (lambda b, hs, g: (b, 0, base + g * HEADS_PER_GROUP + hs))
    qc, kc, vc = (ssm_width // HEAD_DIM, (ssm_width + qkv_width) // HEAD_DIM,
                  (ssm_width + 2 * qkv_width) // HEAD_DIM)
    return pl.pallas_call(
        functools.partial(_attn_prompt_kernel, seq=seq, qb=qb, scale=HEAD_DIM ** -0.5),
        grid=(bsz, HEADS_PER_GROUP, len(DILATIONS)),
        in_specs=[pl.BlockSpec(memory_space=pltpu.SMEM),
                  pl.BlockSpec((1, seq, HEAD_DIM), blk(qc)),
                  pl.BlockSpec((1, seq, HEAD_DIM), blk(kc)),
                  pl.BlockSpec((1, seq, HEAD_DIM), blk(vc))],
        out_specs=pl.BlockSpec((1, seq, HEAD_DIM), lambda b, hs, g: (b, 0, hs)),
        out_shape=jax.ShapeDtypeStruct((bsz, seq, HEADS_PER_GROUP * HEAD_DIM), F32),
        scratch_shapes=[pltpu.VMEM((seq, HEAD_DIM), F32), pltpu.VMEM((seq, HEAD_DIM), F32)],
        compiler_params=_cparams(("parallel", "parallel", "arbitrary")),
        name="attn_prompt",
    )(slopes, z3, z3, z3)


def _attn_sample_kernel(slopes_ref, q_ref, k_ref, v_ref, c0_ref, c1_ref, c2_ref, o_ref, *, n_new, scale):
    caches = (c0_ref, c1_ref, c2_ref)
    for hs in range(HEADS_PER_GROUP):
        pieces = []
        for gi, d in enumerate(DILATIONS):
            c_ref = caches[gi]
            win = c_ref.shape[1]
            hcol = (gi * HEADS_PER_GROUP + hs) * HEAD_DIM
            slope = slopes_ref[gi * HEADS_PER_GROUP + hs]
            q = q_ref[0, :, hcol:hcol + HEAD_DIM].astype(BF16)
            k_new = k_ref[0, :, hcol:hcol + HEAD_DIM].astype(BF16)
            v_new = v_ref[0, :, hcol:hcol + HEAD_DIM].astype(BF16)
            k_old = c_ref[0, :, 0, hs, :].astype(BF16)
            v_old = c_ref[0, :, 1, hs, :].astype(BF16)
            for kk, vv, nk, base in ((k_old, v_old, win, win), (k_new, v_new, n_new, 0)):
                s = lax.dot_general(q, kk, NT_DIMS, preferred_element_type=F32) * scale
                row = lax.broadcasted_iota(I32, (n_new, nk), 0)
                col = lax.broadcasted_iota(I32, (n_new, nk), 1)
                offs = base + row - col
                valid = (offs >= 0) & (offs <= win) & ((offs & (d - 1)) == 0)
                s = jnp.where(valid, s - slope * offs.astype(F32), NEG)
                pieces.append((s, vv))
        m = pieces[0][0].max(axis=1, keepdims=True)
        for s, _ in pieces[1:]:
            m = jnp.maximum(m, s.max(axis=1, keepdims=True))
        l = jnp.zeros((n_new, 1), F32)
        acc = jnp.zeros((n_new, HEAD_DIM), F32)
        for s, vv in pieces:
            p = jnp.exp(s - m)
            l = l + p.sum(axis=1, keepdims=True)
            acc = acc + jnp.dot(p.astype(BF16), vv, preferred_element_type=F32)
        o_ref[0, :, hs * HEAD_DIM:(hs + 1) * HEAD_DIM] = acc / l


def _attn_sample(slopes, q, k, v, caches):
    bsz, n_new, qkv_width = q.shape
    assert all(d & (d - 1) == 0 for d in DILATIONS)
    row_spec = lambda a: pl.BlockSpec((1,) + a.shape[1:], lambda b: (b,) + (0,) * (a.ndim - 1))
    return pl.pallas_call(
        functools.partial(_attn_sample_kernel, n_new=n_new, scale=HEAD_DIM ** -0.5),
        grid=(bsz,),
        in_specs=[pl.BlockSpec(memory_space=pltpu.SMEM), row_spec(q), row_spec(k), row_spec(v)]
                 + [row_spec(c) for c in caches],
        out_specs=pl.BlockSpec((1, n_new, HEADS_PER_GROUP * HEAD_DIM), lambda b: (b, 0, 0)),
        out_shape=jax.ShapeDtypeStruct((bsz, n_new, HEADS_PER_GROUP * HEAD_DIM), F32),
        compiler_params=_cparams(("parallel",)),
        name="attn_sample",
    )(slopes, q, k, v, *caches)


def _ssm_prep_kernel(ldt_ref, ar_r_ref, ai_r_ref, ar_c_ref, ai_c_ref, br_ref, bi_ref, cr_ref, ci_ref,
                     kt_ref, e_ref, f_ref, at_ref, *, chunk):
    dt = jnp.exp(ldt_ref[0])

    def discretise(ar, ai):
        mag = jnp.exp(ar * dt)
        return mag * jnp.cos(ai * dt), mag * jnp.sin(ai * dt)

    ar_r, ai_r = ar_r_ref[0], ai_r_ref[0]
    ar_c, ai_c = ar_c_ref[0], ai_c_ref[0]
    abr_r, abi_r = discretise(ar_r, ai_r)
    abr_c, abi_c = discretise(ar_c, ai_c)
    den = ar_c * ar_c + ai_c * ai_c
    f_re = ((abr_c - 1.0) * ar_c + abi_c * ai_c) / den
    f_im = (abi_c * ar_c - (abr_c - 1.0) * ai_c) / den
    b_re, b_im = br_ref[0], bi_ref[0]
    bb_re = f_re * b_re - f_im * b_im
    bb_im = f_re * b_im + f_im * b_re
    c_re, c_im = cr_ref[0], ci_ref[0]

    def powers(re1, im1):
        pr, pi = [jnp.ones_like(re1)], [jnp.zeros_like(re1)]
        for _ in range(chunk):
            pr.append(pr[-1] * re1 - pi[-1] * im1)
            pi.append(pr[-2] * im1 + pi[-1] * re1)
        return pr, pi

    pr_r, pi_r = powers(abr_r, abi_r)
    pr_c, pi_c = powers(abr_c, abi_c)
    hi = lax.Precision.HIGHEST
    for t in range(chunk):
        cp_re = c_re * pr_r[t] - c_im * pi_r[t]
        cp_im = c_re * pi_r[t] + c_im * pr_r[t]
        kt_ref[0, t] = (jnp.dot(cp_re, bb_re, precision=hi, preferred_element_type=F32)
                        - jnp.dot(cp_im, bb_im, precision=hi, preferred_element_type=F32))
        qr, qi = pr_c[chunk - 1 - t], pi_c[chunk - 1 - t]
        e_ref[0, t, 0] = qr * bb_re - qi * bb_im
        e_ref[0, t, 1] = qr * bb_im + qi * bb_re
        f_ref[0, t, 0] = c_re * pr_r[t + 1] - c_im * pi_r[t + 1]
        f_ref[0, t, 1] = -(c_re * pi_r[t + 1] + c_im * pr_r[t + 1])
    at_ref[0, 0:1, :] = pr_r[chunk]
    at_ref[0, 1:2, :] = pi_r[chunk]


def _ssm_prep(log_dt, a_re, a_im, b_re, b_im, c_re, c_im, chunk):
    g, p = a_re.shape
    h = b_re.shape[2]
    g3 = lambda *s: pl.BlockSpec((1,) + s, lambda i: (i,) + (0,) * len(s))
    return pl.pallas_call(
        functools.partial(_ssm_prep_kernel, chunk=chunk),
        grid=(g,),
        in_specs=[g3(1, 1), g3(1, p), g3(1, p), g3(p, 1), g3(p, 1), g3(p, h), g3(p, h), g3(h, p), g3(h, p)],
        out_specs=[g3(chunk, h, h), g3(chunk, 2, p, h), g3(chunk, 2, h, p), g3(2, p)],
        out_shape=[jax.ShapeDtypeStruct((g, chunk, h, h), F32),
                   jax.ShapeDtypeStruct((g, chunk, 2, p, h), F32),
                   jax.ShapeDtypeStruct((g, chunk, 2, h, p), F32),
                   jax.ShapeDtypeStruct((g, 2, p), F32)],
        compiler_params=_cparams(("parallel",)),
        name="ssm_prep",
    )(log_dt.reshape(g, 1, 1), a_re.reshape(g, 1, p), a_im.reshape(g, 1, p),
      a_re.reshape(g, p, 1), a_im.reshape(g, p, 1), b_re, b_im, c_re, c_im)


def _ssm_layout(kt, e, f, at, chunk):
    g, _, h, _ = kt.shape
    p = at.shape[2]
    ns = g // SSM_PACK
    eye = jnp.eye(SSM_PACK, dtype=F32)
    lag = jnp.arange(chunk)[None, :] - jnp.arange(chunk)[:, None]
    ktoe = jnp.where((lag >= 0)[None, :, :, None, None],
                     kt[:, jnp.clip(lag, 0, chunk - 1)], 0.0)
    ktoe = ktoe.reshape(ns, SSM_PACK, chunk, chunk, h, h)
    m = jnp.einsum('ogijkh,gq->oighjqk', ktoe, eye).reshape(ns, chunk * SSM_PACK * h, chunk * SSM_PACK * h)
    e6 = e.reshape(ns, SSM_PACK, chunk, 2, p, h)
    em = jnp.einsum('ogirph,gq->oighrqp', e6, eye).reshape(ns, chunk * SSM_PACK * h, 2 * SSM_PACK * p)
    f6 = f.reshape(ns, SSM_PACK, chunk, 2, h, p)
    fm = jnp.einsum('ogirhp,gq->orgpiqh', f6, eye).reshape(ns, 2 * SSM_PACK * p, chunk * SSM_PACK * h)
    am = at.reshape(ns, SSM_PACK, 2, p).transpose(0, 2, 1, 3).reshape(ns, 2, SSM_PACK * p)
    return m.astype(BF16), em, fm.astype(BF16), am


def _ssm_kernel(u_ref, m_ref, e_ref, f_ref, a_ref, h0_ref, y_ref, hfin_ref, h_scr, s_scr, hp_scr,
                *, bt, n_chunks, exact_state):
    half = a_ref.shape[2]

    @pl.when(pl.program_id(1) == 0)
    def _():
        h_scr[...] = h0_ref[0]

    u = u_ref[...]
    if exact_state:
        s_scr[...] = jnp.dot(u, e_ref[0], precision=lax.Precision.HIGHEST, preferred_element_type=F32)
    else:
        s_scr[...] = jnp.dot(u.astype(BF16), e_ref[0].astype(BF16), preferred_element_type=F32)
    a_re = jnp.broadcast_to(a_ref[0, 0:1, :], (bt, half))
    a_im = jnp.broadcast_to(a_ref[0, 1:2, :], (bt, half))

    def step(c, carry):
        h_re, h_im = carry
        rows = pl.ds(pl.multiple_of(c * bt, bt), bt)
        hp_scr[rows, 0:half] = h_re
        hp_scr[rows, half:2 * half] = h_im
        n_re = a_re * h_re - a_im * h_im + s_scr[rows, 0:half]
        n_im = a_re * h_im + a_im * h_re + s_scr[rows, half:2 * half]
        return n_re, n_im

    h_re, h_im = lax.fori_loop(0, n_chunks, step, (h_scr[:, 0:half], h_scr[:, half:2 * half]))
    h_scr[:, 0:half] = h_re
    h_scr[:, half:2 * half] = h_im
    hfin_ref[0] = h_scr[...]
    y_ref[...] = (jnp.dot(u.astype(BF16), m_ref[0], preferred_element_type=F32)
                  + jnp.dot(hp_scr[...].astype(BF16), f_ref[0], preferred_element_type=F32)).astype(y_ref.dtype)


def _ssm(u2, m, em, fm, am, h0, bt, exact_state):
    rows, _ = u2.shape
    ns, sw, _ = m.shape
    st = em.shape[2]
    tr = _pick_tile(rows // bt, 64) * bt
    slab = lambda i, r: (i, 0, 0)
    return pl.pallas_call(
        functools.partial(_ssm_kernel, bt=bt, n_chunks=tr // bt, exact_state=exact_state),
        grid=(ns, rows // tr),
        in_specs=[pl.BlockSpec((tr, sw), lambda i, r: (r, i)),
                  pl.BlockSpec((1, sw, sw), slab), pl.BlockSpec((1, sw, st), slab),
                  pl.BlockSpec((1, st, sw), slab), pl.BlockSpec((1, 2, st // 2), slab),
                  pl.BlockSpec((1, bt, st), slab)],
        out_specs=[pl.BlockSpec((tr, sw), lambda i, r: (r, i)), pl.BlockSpec((1, bt, st), slab)],
        out_shape=[jax.ShapeDtypeStruct((rows, ns * sw), BF16), jax.ShapeDtypeStruct((ns, bt, st), F32)],
        scratch_shapes=[pltpu.VMEM((bt, st), F32), pltpu.VMEM((tr, st), F32), pltpu.VMEM((tr, st), F32)],
        compiler_params=_cparams(("parallel", "arbitrary")),
        name="ssm_scan",
    )(u2, m, em, fm, am, h0)


def _to_chunk_rows(u, chunk, lanes):
    b, l, w = u.shape
    c = l // chunk
    return u.reshape(b, c, chunk, w // lanes, lanes).transpose(1, 0, 3, 2, 4).reshape(c * b, chunk * w)


def _from_chunk_rows(y2, b, chunk, lanes):
    rows, cw = y2.shape
    c, w = rows // b, cw // chunk
    return y2.reshape(c, b, w // lanes, chunk, lanes).transpose(1, 0, 3, 2, 4).reshape(b * c * chunk, w)


def _state_to_slabs(h, ns):
    b, g, p, _ = h.shape
    return h.reshape(b, ns, SSM_PACK, p, 2).transpose(1, 0, 4, 2, 3).reshape(ns, b, 2 * SSM_PACK * p)


def _state_from_slabs(hs, p):
    ns, b, _ = hs.shape
    return hs.reshape(ns, b, 2, SSM_PACK, p).transpose(1, 0, 3, 4, 2).reshape(b, ns * SSM_PACK, p, 2)


def _merge_kernel(yc_ref, u_ref, gs_ref, ga_ref, at_ref, x_ref, d_ref, wglu_ref, wbs_ref, wba_ref, wout_ref,
                  g1_ref, b1_ref, h_ref):
    y = yc_ref[...].astype(F32) + d_ref[...] * u_ref[...]
    gl = _gelu_tanh(y)
    so = gl * _sigmoid(jnp.dot(gl.astype(BF16), wglu_ref[...], preferred_element_type=F32))
    merged = (_sigmoid(gs_ref[...]) * jnp.dot(so.astype(BF16), wbs_ref[...], preferred_element_type=F32)
              + _sigmoid(ga_ref[...]) * jnp.dot(at_ref[...].astype(BF16), wba_ref[...], preferred_element_type=F32))
    mix = jnp.dot(merged.astype(BF16), wout_ref[...], preferred_element_type=F32)
    h_ref[...] = _layer_norm(DEEPNORM_ALPHA * x_ref[...] + mix, g1_ref[...], b1_ref[...])


def _merge(yc, z, gates, attn, x2d, d_skip, w_glu, w_bs, w_ba, w_out, ln_g, ln_b, tm):
    n, dm = x2d.shape
    aw = attn.shape[1]
    ssm_width = yc.shape[1]
    row = lambda w, cb=0: pl.BlockSpec((tm, w), lambda i: (i, cb))
    full = lambda a: pl.BlockSpec(a.shape, lambda i: (0, 0), pipeline_mode=pl.Buffered(1))
    return pl.pallas_call(
        _merge_kernel,
        grid=(n // tm,),
        in_specs=[row(ssm_width), row(ssm_width), row(dm, 0), row(dm, 1), row(aw), row(dm),
                  full(d_skip), full(w_glu), full(w_bs), full(w_ba), full(w_out), full(ln_g), full(ln_b)],
        out_specs=row(dm),
        out_shape=jax.ShapeDtypeStruct((n, dm), F32),
        compiler_params=_cparams(("parallel",)),
        name="merge",
    )(yc, z, gates, gates, attn, x2d, d_skip, w_glu, w_bs, w_ba, w_out, ln_g, ln_b)


def _order(x, y):
    if y is None:
        return x, None
    if x is None:
        return y, None
    (va, ta), (vb, tb) = x, y
    first = (va > vb) | ((va == vb) & (ta < tb))
    return ((jnp.where(first, va, vb), jnp.where(first, ta, tb)),
            (jnp.where(first, vb, va), jnp.where(first, tb, ta)))


def _bitonic_merge(xs):
    n = len(xs)
    if n == 1:
        return xs
    pairs = [_order(xs[i], xs[i + n // 2]) for i in range(n // 2)]
    return _bitonic_merge([p[0] for p in pairs]) + _bitonic_merge([p[1] for p in pairs])


def _sort_desc(xs):
    n = len(xs)
    if n == 1:
        return xs
    return _bitonic_merge(_sort_desc(xs[:n // 2]) + _sort_desc(xs[n // 2:])[::-1])


def _top_of_two(xs, ys):
    n = len(xs)
    return _bitonic_merge([_order(xs[i], ys[n - 1 - i])[0] for i in range(n)])


def _top_sorted(items, k):
    items = items + [None] * (-len(items) % k)
    group = lambda s: _sort_desc([it() if callable(it) else it for it in items[s:s + k]])
    run = group(0)
    for s in range(k, len(items), k):
        run = _top_of_two(run, group(s))
    return run


def _peer_topk_kernel(h_ref, wq_ref, k1_ref, k2_ref, a_ref, b_ref, g_ref, s_scr, o_scr, *, half, n_keys, tm):
    k = PEER_TOPK
    nj = tm // 128
    key_bits = n_keys.bit_length() - 1
    hi = lax.Precision.HIGHEST
    def half_top(keys_ref, qh):
        s = lax.dot_general(keys_ref[...], qh, NT_DIMS, precision=hi, preferred_element_type=F32)
        for kh in range(n_keys // 8):
            for j in range(nj):
                s_scr[pl.ds((kh * nj + j) * 8, 8), :] = s[kh * 8:(kh + 1) * 8, j * 128:(j + 1) * 128]
        def item(key):
            start = (key // 8) * nj * 8 + key % 8
            rows = pl.ds(start, nj, stride=8) if nj > 1 else pl.ds(start, 1)
            return lambda: (s_scr[rows, :], jnp.full((nj, 128), key, I32))

        return _top_sorted([item(key) for key in range(n_keys)], k)

    def head(hd, carry):
        cols = pl.ds(pl.multiple_of(hd * 2 * half, 2 * half), 2 * half)
        q = jnp.dot(h_ref[...].astype(BF16), wq_ref[:, cols], preferred_element_type=F32)
        t1 = half_top(k1_ref, q[:, :half])
        t2 = half_top(k2_ref, q[:, half:])
        cands = []
        for r in range(k):
            for c in range(k):
                if (r + 1) * (c + 1) <= k:
                    tag = (t1[r][1] << key_bits) + t2[c][1] + ((r * k + c) << (2 * key_bits))
                    cands.append((t1[r][0] + t2[c][0], tag))
        top = _top_sorted(cands, k)
        ex = [jnp.exp(v - top[0][0]) for v, _ in top]
        tot = ex[0]
        for e in ex[1:]:
            tot = tot + e
        for i, (_, tag) in enumerate(top):
            row = pl.ds(pl.multiple_of((hd * k + i) * nj, nj), nj)
            o_scr[0, row, :] = ((tag >> key_bits) & (n_keys - 1)).astype(F32)
            o_scr[1, row, :] = (tag & (n_keys - 1)).astype(F32)
            o_scr[2, row, :] = ex[i] / tot
        return carry

    lax.fori_loop(0, PEER_HEADS, head, 0)
    ne = PEER_HEADS * k
    for j in range(nj):
        rows = pl.ds(j, ne, stride=nj) if nj > 1 else pl.ds(0, ne)
        tok = pl.ds(j * 128, 128)
        a_ref[tok, :] = o_scr[0, rows, :].T.astype(I32)
        b_ref[tok, :] = o_scr[1, rows, :].T.astype(I32)
        g_ref[tok, :] = o_scr[2, rows, :].T


def _peer_topk(h, wq_bf16, k1, k2, tm):
    n, dm = h.shape
    n_keys, half = k1.shape
    ne = PEER_HEADS * PEER_TOPK
    assert n_keys & (n_keys - 1) == 0 and tm % 128 == 0 and ne == 128
    assert (PEER_TOPK * PEER_TOPK) << (2 * (n_keys.bit_length() - 1)) < 2 ** 31
    full = lambda a: pl.BlockSpec(a.shape, lambda i: (0, 0), pipeline_mode=pl.Buffered(1))
    out = lambda: pl.BlockSpec((tm, ne), lambda i: (i, 0))
    return pl.pallas_call(
        functools.partial(_peer_topk_kernel, half=half, n_keys=n_keys, tm=tm),
        grid=(n // tm,),
        in_specs=[pl.BlockSpec((tm, dm), lambda i: (i, 0)), full(wq_bf16), full(k1), full(k2)],
        out_specs=[out(), out(), out()],
        out_shape=[jax.ShapeDtypeStruct((n, ne), I32), jax.ShapeDtypeStruct((n, ne), I32),
                   jax.ShapeDtypeStruct((n, ne), F32)],
        scratch_shapes=[pltpu.VMEM((n_keys * tm // 128, 128), F32), pltpu.VMEM((3, ne * tm // 128, 128), F32)],
        compiler_params=_cparams(("parallel",)),
        name="peer_topk",
    )(h, wq_bf16, k1, k2)


def _peer_dense_kernel(a_ref, b_ref, g_ref, h_ref, u_ref, v_ref, g2_ref, b2_ref, y_ref, w_scr, x_scr, acc_scr, p_scr,
                       *, tm, n_keys, rows_per_step):
    e_blk = pl.program_id(1)
    n_e = pl.num_programs(1) - 1
    half = tm // 2
    ne = a_ref.shape[1]
    u32 = jnp.uint32

    @pl.when(e_blk == 0)
    def _():
        x_scr[...] = h_ref[...].astype(BF16)
        acc_scr[...] = jnp.zeros_like(acc_scr)
        p_scr[1] = jnp.zeros(p_scr.shape[1:], BF16)
        key_iota = lax.broadcasted_iota(I32, (n_keys, ne), 0)

        def gate_bits(t):
            a_row = a_ref[pl.ds(t, 1), :]
            b_row = b_ref[pl.ds(t, 1), :]
            g_row = g_ref[pl.ds(t, 1), :]
            sel_a = jnp.where(key_iota == a_row, 1.0, 0.0).astype(BF16)
            sel_b = jnp.where(key_iota == b_row, g_row, 0.0).astype(BF16)
            w = lax.dot_general(sel_a, sel_b, NT_DIMS, preferred_element_type=F32)
            return lax.bitcast_convert_type(w.astype(BF16).astype(F32), u32)

        def build(t, carry):
            word = gate_bits(t) | (gate_bits(t + half) >> 16)
            w_scr[pl.ds(pl.multiple_of(t * n_keys, n_keys), n_keys), :] = word
            return carry

        lax.fori_loop(0, half, build, 0, unroll=8)

    slot = e_blk % 2
    acc_scr[...] += jnp.dot(p_scr[1 - slot], v_ref[...], preferred_element_type=F32)

    z = lax.dot_general(x_scr[...], u_ref[...], NT_DIMS, preferred_element_type=F32)
    first_row = jnp.minimum(e_blk, n_e - 1) * rows_per_step
    for c in range(rows_per_step):
        word = w_scr[pl.ds(first_row + c, half, stride=n_keys), :]
        w_a = lax.bitcast_convert_type(word & u32(0xFFFF0000), F32)
        w_b = lax.bitcast_convert_type(word << 16, F32)
        cols = slice(c * n_keys, (c + 1) * n_keys)
        p_scr[slot, 0:half, cols] = (w_a * _gelu_tanh(z[0:half, cols])).astype(BF16)
        p_scr[slot, half:tm, cols] = (w_b * _gelu_tanh(z[half:tm, cols])).astype(BF16)

    @pl.when(e_blk == n_e)
    def _():
        y_ref[...] = _layer_norm(DEEPNORM_ALPHA * h_ref[...] + acc_scr[...], g2_ref[...], b2_ref[...])


def _peer_dense(a, b, g, h, u_bf16, v_bf16, ln_g, ln_b, n_keys, tm, rows_per_step):
    n, dm = h.shape
    ne = a.shape[1]
    eb = rows_per_step * n_keys
    n_e = n_keys // rows_per_step
    tok = lambda w: pl.BlockSpec((tm, w), lambda i, e: (i, 0))
    full = lambda arr: pl.BlockSpec(arr.shape, lambda i, e: (0, 0))
    return pl.pallas_call(
        functools.partial(_peer_dense_kernel, tm=tm, n_keys=n_keys, rows_per_step=rows_per_step),
        grid=(n // tm, n_e + 1),
        in_specs=[tok(ne), tok(ne), tok(ne), tok(dm),
                  pl.BlockSpec((eb, dm), lambda i, e: (jnp.minimum(e, n_e - 1), 0)),
                  pl.BlockSpec((eb, dm), lambda i, e: (jnp.maximum(e - 1, 0), 0)),
                  full(ln_g), full(ln_b)],
        out_specs=tok(dm),
        out_shape=jax.ShapeDtypeStruct((n, dm), F32),
        scratch_shapes=[pltpu.VMEM((tm // 2 * n_keys, n_keys), jnp.uint32), pltpu.VMEM((tm, dm), BF16),
                        pltpu.VMEM((tm, dm), F32), pltpu.VMEM((2, tm, eb), BF16)],
        compiler_params=_cparams(("parallel", "arbitrary")),
        name="peer_experts",
    )(a, b, g, h, u_bf16, v_bf16, ln_g, ln_b)


def _layer(x, caches, h0, wts, ssm_mats):
    (w_in, ssm_d, w_glu, w_bs, w_ba, w_out, ln1_g, ln1_b, wq, k1, k2, pu, pv, ln2_g, ln2_b, slopes) = wts
    m, em, fm, am = ssm_mats
    bsz, seq, dm = x.shape
    n = bsz * seq
    ssm_width = ssm_d.shape[1]
    in_width = w_in.shape[1]
    qkv_width = (in_width - ssm_width - 2 * dm) // 3
    kv_half = HEADS_PER_GROUP * HEAD_DIM
    k_off, v_off = ssm_width + qkv_width, ssm_width + 2 * qkv_width
    x2d = x.reshape(n, dm)

    gate_off = ssm_width + 3 * qkv_width
    z = _proj(x2d, w_in[:, :gate_off], _pick_tile(n, 1024), _pick_tile(gate_off, 512))
    gates = _proj(x2d, w_in[:, gate_off:], _pick_tile(n, 1024), _pick_tile(2 * dm, 1024))
    z3 = z.reshape(bsz, seq, gate_off)

    new_kv = []
    if caches is None:
        qb = seq // max(DILATIONS)
        attn = _attn_prompt(z3, slopes, ssm_width, qkv_width, qb).reshape(n, kv_half)
        for gi, d in enumerate(DILATIONS):
            keep = min(qb * d, seq)
            kg = z3[:, seq - keep:, k_off + gi * kv_half:k_off + (gi + 1) * kv_half]
            vg = z3[:, seq - keep:, v_off + gi * kv_half:v_off + (gi + 1) * kv_half]
            new_kv.append(jnp.stack([kg, vg], axis=2).reshape(bsz, keep, 2, HEADS_PER_GROUP, HEAD_DIM))
    else:
        q = z3[:, :, ssm_width:k_off]
        k = z3[:, :, k_off:v_off]
        v = z3[:, :, v_off:v_off + qkv_width]
        attn = _attn_sample(slopes, q, k, v, list(caches)).reshape(n, kv_half)
        for gi in range(len(DILATIONS)):
            kg = k[:, :, gi * kv_half:(gi + 1) * kv_half]
            vg = v[:, :, gi * kv_half:(gi + 1) * kv_half]
            new_kv.append(jnp.stack([kg, vg], axis=2).reshape(bsz, seq, 2, HEADS_PER_GROUP, HEAD_DIM))

    ns = m.shape[0]
    p_state = em.shape[2] // (2 * SSM_PACK)
    lanes = ssm_width // ns
    u3 = z3[:, :, :ssm_width]
    exact_state = h0 is not None
    u2 = _to_chunk_rows(u3 if exact_state else u3.astype(BF16), SSM_CHUNK, lanes)
    h0s = (_state_to_slabs(h0, ns) if exact_state
           else jnp.zeros((ns, bsz, 2 * SSM_PACK * p_state), F32))
    y2, hfin = _ssm(u2, m, em, fm, am, h0s, bsz, exact_state)
    yc = _from_chunk_rows(y2, bsz, SSM_CHUNK, lanes)
    ssm_state = _state_from_slabs(hfin, p_state)

    h = _merge(yc, z, gates, attn, x2d, ssm_d, w_glu, w_bs, w_ba, w_out, ln1_g, ln1_b, _pick_tile(n, 256))
    a, b, g = _peer_topk(h, wq, k1, k2, _pick_tile(n, 1024))
    y = _peer_dense(a, b, g, h, pu, pv, ln2_g, ln2_b, k1.shape[0], _pick_tile(n, 512), 4)
    return y.reshape(bsz, seq, dm), new_kv, ssm_state


def kernel(x_prompt, x_sample, cache_kv_w128, cache_kv_w512, cache_kv_w2048, state_ssm, w_in, ssm_log_dt, ssm_a_re, ssm_a_im, ssm_b_re, ssm_b_im, ssm_c_re, ssm_c_im, ssm_d, ssm_w_glu, w_branch_ssm, w_branch_attn, w_out, ln1_g, ln1_b, peer_w_q, peer_sub_keys_1, peer_sub_keys_2, peer_u, peer_v, ln2_g, ln2_b):
    caches = (cache_kv_w128, cache_kv_w512, cache_kv_w2048)
    seq = x_prompt.shape[1]
    assert x_sample.shape[1] == SSM_CHUNK and seq % (SSM_CHUNK * max(DILATIONS)) == 0
    assert all(c.shape[1] == (seq // max(DILATIONS)) * d and c.shape[1] <= PAST_LEN
               for c, d in zip(caches, DILATIONS))
    n_heads = len(DILATIONS) * HEADS_PER_GROUP
    slopes = 2.0 ** (-8.0 * jnp.arange(1, n_heads + 1, dtype=F32) / n_heads)
    row = lambda a: a.reshape(1, -1).astype(F32)
    wts = (w_in.astype(BF16), row(ssm_d), ssm_w_glu.astype(BF16), w_branch_ssm.astype(BF16),
           w_branch_attn.astype(BF16), w_out.astype(BF16), row(ln1_g), row(ln1_b), peer_w_q.astype(BF16),
           peer_sub_keys_1.astype(F32), peer_sub_keys_2.astype(F32), peer_u.astype(BF16), peer_v.astype(BF16),
           row(ln2_g), row(ln2_b), slopes)
    ssm_mats = _ssm_layout(*_ssm_prep(ssm_log_dt, ssm_a_re, ssm_a_im, ssm_b_re, ssm_b_im, ssm_c_re, ssm_c_im,
                                      SSM_CHUNK), SSM_CHUNK)
    y_p, kv_p, ssm_p = _layer(x_prompt, None, None, wts, ssm_mats)
    y_s, kv_s, ssm_s = _layer(x_sample, caches, state_ssm, wts, ssm_mats)
    return (y_p, y_s, kv_p[0], kv_p[1], kv_p[2], ssm_p, kv_s[0], kv_s[1], kv_s[2], ssm_s)
```
